```python
import jax, jax.numpy as jnp
from jax import lax
import numpy as np

D_MODEL = 1024
BATCH = 2
SEQ = 8192
DEPTH = 1
DEC_BATCH = 4
DEC_SEQ = 8192
PAST_LEN = 128

HG_HEADS = 8
HG_KDIM = 128
HG_VDIM = D_MODEL // HG_HEADS
HG_FDIM = HG_HEADS * HG_KDIM
HG_IDIM = HG_HEADS * HG_VDIM
CHUNK = 64
ATTN_GROUPS = ((128, 1), (512, 4), (2048, 16))
ATTN_HEADS = 4
ATTN_HEAD_DIM = 128
ATTN_WIDTH = ATTN_HEADS * ATTN_HEAD_DIM
ROT_DIM = ATTN_HEAD_DIM // 4
ROPE_THETA = 500000.0
N_MEM = 256
XA_HEADS = 4
XA_HEAD_DIM = D_MODEL // XA_HEADS
D_FF = 4 * D_MODEL
RMS_EPS = 1e-6

IN_SIZES = (HG_FDIM, HG_FDIM, HG_FDIM, HG_IDIM, HG_IDIM) + (ATTN_WIDTH,) * (3 * len(ATTN_GROUPS)) + (D_MODEL, D_MODEL)
N_IN = sum(IN_SIZES)
IN_SPLITS = [int(c) for c in np.cumsum(IN_SIZES)[:-1]]

kernel_name = "hgrn2_dilated_attn_parallel_encoder"


def _rmsnorm(x, g):
    xf = x.astype(jnp.float32)
    y = xf * lax.rsqrt(jnp.mean(xf * xf, axis=-1, keepdims=True) + RMS_EPS)
    return (y * g.astype(jnp.float32)).astype(x.dtype)


def _gla_chunk_scan(q, k, logf, v):
    B, H, S, K = q.shape
    V = v.shape[-1]
    n = S // CHUNK

    def chunks(t):
        return t.reshape(B, H, n, CHUNK, t.shape[-1]).transpose(2, 0, 1, 3, 4)

    causal = jnp.tril(jnp.ones((CHUNK, CHUNK), dtype=bool))[:, :, None]

    def step(S0, xs):
        qb, kb, fb, vb = xs
        b = jnp.cumsum(fb, axis=2)
        o_inter = jnp.einsum('bhck,bhkv->bhcv', qb * jnp.exp(b), S0)
        diff = b[:, :, :, None, :] - b[:, :, None, :, :]
        decay = jnp.exp(jnp.where(causal, diff, -jnp.inf))
        attn = jnp.einsum('bhtk,bhsk,bhtsk->bhts', qb, kb, decay)
        o_intra = jnp.einsum('bhts,bhsv->bhtv', attn, vb)
        b_last = b[:, :, -1:, :]
        S_new = jnp.exp(b_last[:, :, 0, :])[..., None] * S0 + jnp.einsum(
            'bhsk,bhsv->bhkv', kb * jnp.exp(b_last - b), vb)
        return S_new, o_inter + o_intra

    S0 = jnp.zeros((B, H, K, V), jnp.float32)
    _, o = lax.scan(step, S0, (chunks(q), chunks(k), chunks(logf), chunks(v)))
    return o.transpose(1, 2, 0, 3, 4).reshape(B, H, S, V)


def _hgrn2_bidir(q, f_fwd, f_bwd, i, g, lb, gnorm_g):
    B, S, _ = q.shape
    f32 = jnp.float32

    def heads(t, d):
        return t.reshape(B, S, HG_HEADS, d).transpose(0, 2, 1, 3).astype(f32)

    qh = jax.nn.silu(heads(q, HG_KDIM))
    vh = heads(i, HG_VDIM)
    lbh = lb.astype(f32).reshape(2, HG_HEADS, 1, HG_KDIM)

    def gate(fpre, lbd):
        fg = lbd + (1.0 - lbd) * jax.nn.sigmoid(heads(fpre, HG_KDIM))
        return 1.0 - fg, jnp.log(fg)

    k_f, lf_f = gate(f_fwd, lbh[0])
    k_b, lf_b = gate(f_bwd, lbh[1])
    o_f = _gla_chunk_scan(qh, k_f, lf_f, vh)
    rev = lambda t: jnp.flip(t, axis=2)
    o_b = rev(_gla_chunk_scan(rev(qh), rev(k_b), rev(lf_b), rev(vh)))
    o = (o_f + o_b).transpose(0, 2, 1, 3)
    o = _rmsnorm(o, gnorm_g) * jax.nn.silu(g.reshape(B, S, HG_HEADS, HG_VDIM).astype(f32))
    return o.reshape(B, S, HG_IDIM).astype(q.dtype)


def _partial_rotary(t, pos):
    half = ROT_DIM // 2
    inv = ROPE_THETA ** (-jnp.arange(half, dtype=jnp.float32) * 2.0 / ROT_DIM)
    ang = pos.astype(jnp.float32)[:, None] * inv[None, :]
    cos = jnp.cos(ang)[None, :, None, :]
    sin = jnp.sin(ang)[None, :, None, :]
    tr = t[..., :ROT_DIM].astype(jnp.float32)
    x1, x2 = tr[..., :half], tr[..., half:]
    rot = jnp.concatenate([x1 * cos - x2 * sin, x2 * cos + x1 * sin], axis=-1)
    return jnp.concatenate([rot.astype(t.dtype), t[..., ROT_DIM:]], axis=-1)


def _dilated_window_attention(q, k, v, span, dil):
    B, S, H, Dh = q.shape
    L = S // dil
    N = B * dil
    blk = span
    nb = -(-L // blk)
    Lp = nb * blk

    def residue(t):
        return t.reshape(B, L, dil, H, Dh).transpose(0, 2, 1, 3, 4).reshape(N, L, H, Dh)

    qr, kr, vr = residue(q), residue(k), residue(v)
    qb = jnp.pad(qr, ((0, 0), (0, Lp - L), (0, 0), (0, 0))).reshape(N, nb, blk, H, Dh)

    def kv_blocks(t):
        tp = jnp.pad(t, ((0, 0), (blk, Lp - L + blk), (0, 0), (0, 0))).reshape(N, nb + 2, blk, H, Dh)
        return jnp.concatenate([tp[:, :-2], tp[:, 1:-1], tp[:, 2:]], axis=2)

    kb, vb = kv_blocks(kr), kv_blocks(vr)
    m_q = jnp.arange(nb)[:, None, None] * blk + jnp.arange(blk)[None, :, None]
    m_k = jnp.arange(nb)[:, None, None] * blk - blk + jnp.arange(3 * blk)[None, None, :]
    mask = ((jnp.abs(m_k - m_q) <= span) & (m_k >= 0) & (m_k < L)) | (m_k == m_q)
    s = jnp.einsum('nbqhd,nbkhd->nbhqk', qb, kb).astype(jnp.float32) * (Dh ** -0.5)
    s = jnp.where(mask[None, :, None], s, -jnp.inf)
    lse = jax.nn.logsumexp(s, axis=-1)
    p = jnp.exp(s - lse[..., None])
    o = jnp.einsum('nbhqk,nbkhd->nbqhd', p.astype(v.dtype), vb).reshape(N, Lp, H, Dh)[:, :L]
    lse = lse.transpose(0, 1, 3, 2).reshape(N, Lp, H)[:, :L]
    o = o.reshape(B, dil, L, H, Dh).transpose(0, 2, 1, 3, 4).reshape(B, S, H, Dh)
    lse = lse.reshape(B, dil, L, H).transpose(0, 2, 1, 3).reshape(B, S, H)
    return o, lse


def _memory_cross_attention(u, mem_n, w_q, w_kv, w_o):
    B, S, _ = u.shape
    M = mem_n.shape[1]
    q = (u @ w_q).reshape(B, S, XA_HEADS, XA_HEAD_DIM)
    k, v = jnp.split(mem_n @ w_kv, 2, axis=-1)
    k = k.reshape(B, M, XA_HEADS, XA_HEAD_DIM)
    v = v.reshape(B, M, XA_HEADS, XA_HEAD_DIM)
    s = jnp.einsum('bshd,bmhd->bhsm', q, k).astype(jnp.float32) * (XA_HEAD_DIM ** -0.5)
    p = jax.nn.softmax(s, axis=-1)
    o = jnp.einsum('bhsm,bmhd->bshd', p.astype(v.dtype), v).reshape(B, S, D_MODEL)
    return o @ w_o


def _encode(x, mem, mix_norm_g, w_in, hgrn_lb_logits, hgrn_gnorm_g, w_hgrn_o, w_attn_o, w_out,
            xa_norm_g, mem_norm_g, w_xq, w_xkv, w_xo, ffn_norm_g, w_ffn1, w_ffn2, final_norm_g):
    B, S, _ = x.shape
    pos = jnp.arange(S)
    lb_all = jnp.cumsum(jax.nn.softmax(hgrn_lb_logits.astype(jnp.float32), axis=1), axis=1)
    h = x
    for l in range(DEPTH):
        u = _rmsnorm(h, mix_norm_g[l])
        parts = jnp.split(u @ w_in[l], IN_SPLITS, axis=-1)
        hq, hf_f, hf_b, hi, hg = parts[:5]
        attn_parts = parts[5:5 + 3 * len(ATTN_GROUPS)]
        gate_h, gate_a = parts[5 + 3 * len(ATTN_GROUPS):]

        y_h = _hgrn2_bidir(hq, hf_f, hf_b, hi, hg, lb_all[:, l], hgrn_gnorm_g[l]) @ w_hgrn_o[l]

        outs, lses = [], []
        for gi, (win, dil) in enumerate(ATTN_GROUPS):
            qg, kg, vg = [t.reshape(B, S, ATTN_HEADS, ATTN_HEAD_DIM) for t in attn_parts[3 * gi:3 * gi + 3]]
            o, lse = _dilated_window_attention(_partial_rotary(qg, pos), _partial_rotary(kg, pos), vg,
                                               (win // 2) // dil, dil)
            outs.append(o)
            lses.append(lse)
        wts = jax.nn.softmax(jnp.stack(lses), axis=0)
        attn = jnp.einsum('gbsh,gbshd->bshd', wts, jnp.stack(outs).astype(jnp.float32))
        y_a = attn.reshape(B, S, ATTN_WIDTH).astype(h.dtype) @ w_attn_o[l]

        merged = jax.nn.sigmoid(gate_h) * y_h + jax.nn.sigmoid(gate_a) * y_a
        h = h + merged @ w_out[l]
        h = h + _memory_cross_attention(_rmsnorm(h, xa_norm_g[l]), _rmsnorm(mem, mem_norm_g[l]),
                                        w_xq[l], w_xkv[l], w_xo[l])
        u = _rmsnorm(h, ffn_norm_g[l])
        h = h + jnp.square(jax.nn.relu(u @ w_ffn1[l])) @ w_ffn2[l]
    return _rmsnorm(h, final_norm_g)


def setup_inputs(seed: int = 0) -> dict:
    key = jax.random.key(seed)
    ks = jax.random.split(key, 24)
    f32 = jnp.float32

    def w(k, shape, fan_in):
        return jax.random.normal(k, shape, f32) * (fan_in ** -0.5)

    def gain(k, shape):
        return 1.0 + 0.02 * jax.random.normal(k, shape, f32)

    return {
        "x_prompt": jax.random.normal(ks[0], (BATCH, SEQ, D_MODEL), f32),
        "x_sample": jax.random.normal(ks[1], (DEC_BATCH, DEC_SEQ, D_MODEL), f32),
        "mem_prompt": jax.random.normal(ks[2], (BATCH, N_MEM, D_MODEL), f32),
        "mem_sample": jax.random.normal(ks[3], (DEC_BATCH, N_MEM, D_MODEL), f32),
        "mix_norm_g": gain(ks[4], (DEPTH, D_MODEL)),
        "w_in": w(ks[5], (DEPTH, D_MODEL, N_IN), D_MODEL),
        "hgrn_lb_logits": 0.5 * jax.random.normal(ks[6], (2, DEPTH + 1, HG_FDIM), f32),
        "hgrn_gnorm_g": gain(ks[7], (DEPTH, HG_VDIM)),
        "w_hgrn_o": w(ks[8], (DEPTH, HG_IDIM, D_MODEL), HG_IDIM),
        "w_attn_o": w(ks[9], (DEPTH, ATTN_WIDTH, D_MODEL), ATTN_WIDTH),
        "w_out": w(ks[10], (DEPTH, D_MODEL, D_MODEL), D_MODEL),
        "xa_norm_g": gain(ks[11], (DEPTH, D_MODEL)),
        "mem_norm_g": gain(ks[12], (DEPTH, D_MODEL)),
        "w_xq": w(ks[13], (DEPTH, D_MODEL, XA_HEADS * XA_HEAD_DIM), D_MODEL),
        "w_xkv": w(ks[14], (DEPTH, D_MODEL, 2 * XA_HEADS * XA_HEAD_DIM), D_MODEL),
        "w_xo": w(ks[15], (DEPTH, XA_HEADS * XA_HEAD_DIM, D_MODEL), XA_HEADS * XA_HEAD_DIM),
        "ffn_norm_g": gain(ks[16], (DEPTH, D_MODEL)),
        "w_ffn1": w(ks[17], (DEPTH, D_MODEL, D_FF), D_MODEL),
        "w_ffn2": w(ks[18], (DEPTH, D_FF, D_MODEL), D_FF),
        "final_norm_g": gain(ks[19], (D_MODEL,)),
    }


def reference(x_prompt, x_sample, mem_prompt, mem_sample, mix_norm_g, w_in, hgrn_lb_logits, hgrn_gnorm_g,
              w_hgrn_o, w_attn_o, w_out, xa_norm_g, mem_norm_g, w_xq, w_xkv, w_xo, ffn_norm_g, w_ffn1,
              w_ffn2, final_norm_g):
    y_prompt = _encode(x_prompt, mem_prompt, mix_norm_g, w_in, hgrn_lb_logits, hgrn_gnorm_g, w_hgrn_o,
                       w_attn_o, w_out, xa_norm_g, mem_norm_g, w_xq, w_xkv, w_xo, ffn_norm_g, w_ffn1,
                       w_ffn2, final_norm_g)
    y_sample = _encode(x_sample, mem_sample, mix_norm_g, w_in, hgrn_lb_logits, hgrn_gnorm_g, w_hgrn_o,
                       w_attn_o, w_out, xa_norm_g, mem_norm_g, w_xq, w_xkv, w_xo, ffn_norm_g, w_ffn1,
                       w_ffn2, final_norm_g)
    return (y_prompt, y_sample)
```

```python
import functools
import math

import numpy as np
import jax
import jax.numpy as jnp
from jax import lax
from jax.experimental import pallas as pl
from jax.experimental.pallas import tpu as pltpu

F32 = jnp.float32
BF16 = jnp.bfloat16

RMS_EPS = 1e-6
ROPE_THETA = 500000.0
HG_HEADS = 8
HEAD = 128
ATTN_HEADS = 4
ATTN_WIDTH = ATTN_HEADS * HEAD
ATTN_GROUPS = ((128, 1), (512, 4), (2048, 16))
ROT_DIM = HEAD // 4
XA_HEADS = 4
CHUNK = 64
SUB = 8
NEG = -1e30

VMEM_LIMIT = 56 * 1024 * 1024


def _cparams(sem):
    return pltpu.CompilerParams(dimension_semantics=sem, vmem_limit_bytes=VMEM_LIMIT)


def _sigmoid(x):
    return 1.0 / (1.0 + jnp.exp(-x))


def _rms_scale(xf):
    return lax.rsqrt(jnp.mean(xf * xf, axis=-1, keepdims=True) + RMS_EPS)


def _ep_plain(acc):
    return acc


def _ep_silu(acc):
    return acc * _sigmoid(acc)


def _ep_sigmoid(acc):
    return _sigmoid(acc)


def _ep_logf(acc, lb):
    return jnp.log(lb + (1.0 - lb) * _sigmoid(acc))


def _ep_rotary(acc, cos, sin_lo, sin_hi, colscale):
    half = ROT_DIM // 2
    outs = []
    for c in range(acc.shape[1] // HEAD):
        t = acc[:, c * HEAD:(c + 1) * HEAD]
        r = t * cos + pltpu.roll(t, HEAD - half, 1) * sin_lo + pltpu.roll(t, half, 1) * sin_hi
        outs.append(r)
    return jnp.concatenate(outs, axis=1) * colscale


def _norm_proj_kernel(*refs, epilogue, n_extra):
    x_ref, g_ref, w_ref = refs[:3]
    extra = refs[3:3 + n_extra]
    o_ref = refs[3 + n_extra]
    xf = x_ref[...]
    u = (xf * _rms_scale(xf) * g_ref[...]).astype(BF16)
    acc = jnp.dot(u, w_ref[...], preferred_element_type=F32)
    o_ref[...] = epilogue(acc, *[e[...] for e in extra]).astype(o_ref.dtype)


def _norm_proj(x2d, g, w, epilogue, out_dtype, extras=(), extra_specs=(), tm=512, name="norm_proj"):
    m, d = x2d.shape
    n = w.shape[1]
    tm = min(tm, m)
    assert m % tm == 0
    in_specs = [
        pl.BlockSpec((tm, d), lambda i: (i, 0)),
        pl.BlockSpec((1, d), lambda i: (0, 0)),
        pl.BlockSpec((d, n), lambda i: (0, 0)),
    ] + list(extra_specs)
    return pl.pallas_call(
        functools.partial(_norm_proj_kernel, epilogue=epilogue, n_extra=len(extras)),
        grid=(m // tm,),
        in_specs=in_specs,
        out_specs=pl.BlockSpec((tm, n), lambda i: (i, 0)),
        out_shape=jax.ShapeDtypeStruct((m, n), out_dtype),
        compiler_params=_cparams(("parallel",)),
        name=name,
    )(x2d, g.reshape(1, d), w, *extras)


def _hgrn_kernel(*refs, rev, final, nchunk):
    if final:
        q_ref, lf_ref, v_ref, tri_ref, of_ref, g_ref, gn_ref, o_ref, st_ref, kc_ref, bc_ref = refs
    else:
        q_ref, lf_ref, v_ref, tri_ref, o_ref, st_ref, kc_ref, bc_ref = refs
    C = CHUNK
    ng = C // SUB

    @pl.when(pl.program_id(2) == 0)
    def _():
        st_ref[...] = jnp.zeros_like(st_ref)
        kc_ref[...] = jnp.zeros_like(kc_ref)
        bc_ref[...] = jnp.zeros_like(bc_ref)

    rowl = lax.broadcasted_iota(jnp.int32, (C, HEAD), 0)
    sub_pos = rowl & (SUB - 1)
    arow = lax.broadcasted_iota(jnp.int32, (C, C), 0)
    acol = lax.broadcasted_iota(jnp.int32, (C, C), 1)
    ones_b = jnp.ones((HEAD, HEAD), BF16)
    zeros_g = jnp.zeros((SUB, HEAD), F32)
    nt = (((1,), (1,)), ((), ()))
    tn = (((0,), (0,)), ((), ()))

    def body(i, carry):
        c = (nchunk - 1 - i) if rev else i
        r0 = pl.multiple_of(c * C, C)
        lf = lf_ref[0, pl.ds(r0, C), :]
        q = q_ref[0, pl.ds(r0, C), :].astype(F32)
        v = v_ref[0, pl.ds(r0, C), :]
        k = 1.0 - jnp.exp(lf)
        hi = lf.astype(BF16)
        r1 = lf - hi.astype(F32)
        mid = r1.astype(BF16)
        lo = (r1 - mid.astype(F32)).astype(BF16)
        b = jnp.dot(tri_ref[...], jnp.concatenate([hi, mid, lo], axis=0), preferred_element_type=F32)
        kc_ref[pl.ds(SUB, C), :] = k
        bc_ref[pl.ds(SUB, C), :] = b
        b_end = bc_ref[pl.ds(SUB + (0 if rev else C - 1), 1), :]

        st = st_ref[...]
        qi = (q * jnp.exp(b)).astype(BF16)
        o = lax.dot_general(qi, st.astype(BF16), nt, preferred_element_type=F32)
        kl = (k * jnp.exp(b_end - b)).astype(BF16)
        st_ref[...] = st * jnp.exp(b_end) + lax.dot_general(v, kl, tn, preferred_element_type=F32)

        a = jnp.zeros((C, C), F32)
        h = C // 2
        while h >= SUB:
            qparts, kparts = [], []
            for gi in range(ng):
                t0 = gi * SUB
                blk = t0 // (2 * h)
                in_upper = (t0 % (2 * h)) >= h
                is_query = (not in_upper) if rev else in_upper
                rr = blk * 2 * h + (h if rev else h - 1)
                bref = bc_ref[pl.ds(SUB + rr, 1), :]
                bg = b[t0:t0 + SUB]
                if is_query:
                    qparts.append(q[t0:t0 + SUB] * jnp.exp(bg - bref))
                    kparts.append(zeros_g)
                else:
                    qparts.append(zeros_g)
                    kparts.append(k[t0:t0 + SUB] * jnp.exp(bref - bg))
            qh = jnp.concatenate(qparts, axis=0).astype(BF16)
            kh = jnp.concatenate(kparts, axis=0).astype(BF16)
            ah = lax.dot_general(qh, kh, nt, preferred_element_type=F32)
            if 2 * h < C:
                ah = jnp.where((arow ^ acol) < 2 * h, ah, 0.0)
            a = a + ah
            h //= 2

        for d in range(SUB):
            sh = SUB + (d if rev else -d)
            ks = kc_ref[pl.ds(sh, C), :]
            bs = bc_ref[pl.ds(sh, C), :]
            ok = (sub_pos + d <= SUB - 1) if rev else (sub_pos >= d)
            p = jnp.where(ok, q * ks * jnp.exp(b - bs), 0.0).astype(BF16)
            rs = jnp.dot(p, ones_b, preferred_element_type=F32)[:, :C]
            tgt = (arow + d) if rev else (arow - d)
            a = a + jnp.where(acol == tgt, rs, 0.0)

        o = o + jnp.dot(a.astype(BF16), v, preferred_element_type=F32)
        if final:
            tot = o + of_ref[0, pl.ds(r0, C), :]
            y = tot * _rms_scale(tot) * gn_ref[...] * g_ref[0, pl.ds(r0, C), :].astype(F32)
            o_ref[0, pl.ds(r0, C), :] = y.astype(o_ref.dtype)
        else:
            o_ref[0, pl.ds(r0, C), :] = o
        return carry

    lax.fori_loop(0, nchunk, body, 0)


def _tri_matrix(rev):
    t = np.arange(CHUNK)
    m = (t[None, :] >= t[:, None]) if rev else (t[None, :] <= t[:, None])
    return jnp.asarray(np.concatenate([m, m, m], axis=1).astype(np.float32), dtype=BF16)


def _hgrn_pass(a_silu, a_lf, a_plain, gnorm, o_fwd, rev, rows):
    bsz, s, _ = a_silu.shape
    rows = min(rows, s)
    nblk = s // rows
    final = o_fwd is not None
    seq = (lambda i: nblk - 1 - i) if rev else (lambda i: i)
    fcol = HG_HEADS if rev else 0
    in_specs = [
        pl.BlockSpec((1, rows, HEAD), lambda b, h, i: (b, seq(i), h)),
        pl.BlockSpec((1, rows, HEAD), lambda b, h, i: (b, seq(i), h + fcol)),
        pl.BlockSpec((1, rows, HEAD), lambda b, h, i: (b, seq(i), h)),
        pl.BlockSpec((CHUNK, 3 * CHUNK), lambda b, h, i: (0, 0)),
    ]
    args = [a_silu, a_lf, a_plain, _tri_matrix(rev)]
    if final:
        in_specs += [
            pl.BlockSpec((1, rows, HEAD), lambda b, h, i: (b, seq(i), h)),
            pl.BlockSpec((1, rows, HEAD), lambda b, h, i: (b, seq(i), h + HG_HEADS)),
            pl.BlockSpec((1, HEAD), lambda b, h, i: (0, 0)),
        ]
        args += [o_fwd, a_silu, gnorm.reshape(1, HEAD)]
    return pl.pallas_call(
        functools.partial(_hgrn_kernel, rev=rev, final=final, nchunk=rows // CHUNK),
        grid=(bsz, HG_HEADS, nblk),
        in_specs=in_specs,
        out_specs=pl.BlockSpec((1, rows, HEAD), lambda b, h, i: (b, seq(i), h)),
        out_shape=jax.ShapeDtypeStruct((bsz, s, HG_HEADS * HEAD), BF16 if final else F32),
        scratch_shapes=[
            pltpu.VMEM((HEAD, HEAD), F32),
            pltpu.VMEM((CHUNK + 2 * SUB, HEAD), F32),
            pltpu.VMEM((CHUNK + 2 * SUB, HEAD), F32),
        ],
        compiler_params=_cparams(("parallel", "parallel", "arbitrary")),
        name="hgrn_bwd" if rev else "hgrn_fwd",
    )(*args)


def _attn_kernel(q_ref, kp_ref, k_ref, kn_ref, vp_ref, v_ref, vn_ref, o_ref, lse_ref, *, tq, span, length):
    m0 = pl.program_id(2) * tq
    qt = 2 * span
    kt = 4 * span
    nt = (((1,), (1,)), ((), ()))
    row = lax.broadcasted_iota(jnp.int32, (qt, kt), 0)
    col = lax.broadcasted_iota(jnp.int32, (qt, kt), 1)
    band = jnp.where(col >= row, jnp.where(col <= row + 2 * span, 0.0, NEG), NEG)
    for h in range(ATTN_HEADS):
        cs = slice(h * HEAD, (h + 1) * HEAD)
        kcat = jnp.concatenate([kp_ref[0, :, cs], k_ref[0, :, cs], kn_ref[0, :, cs]], axis=0)
        vcat = jnp.concatenate([vp_ref[0, :, cs], v_ref[0, :, cs], vn_ref[0, :, cs]], axis=0)
        for j in range(tq // qt):
            q = q_ref[0, j * qt:(j + 1) * qt, cs]
            kw = kcat[j * qt:j * qt + kt]
            vw = vcat[j * qt:j * qt + kt]
            s = lax.dot_general(q, kw, nt, preferred_element_type=F32) + band
            mk = col + (m0 + (j * qt - span))
            s = jnp.where(mk >= 0, jnp.where(mk < length, s, NEG), NEG)
            mx = jnp.max(s, axis=-1, keepdims=True)
            p = jnp.exp(s - mx)
            den = jnp.sum(p, axis=-1, keepdims=True)
            o = jnp.dot(p.astype(BF16), vw, preferred_element_type=F32) * (1.0 / den)
            o_ref[0, j * qt:(j + 1) * qt, cs] = o.astype(o_ref.dtype)
            lse_ref[0, j * qt:(j + 1) * qt, cs] = jnp.broadcast_to(mx + jnp.log(den), (qt, HEAD))


def _dilated_attention(a_rot, a_plain, gi, span, dil, tq=512):
    bsz, s, wr = a_rot.shape
    wp = a_plain.shape[2]
    length = s // dil
    tq = min(tq, length)
    assert length % tq == 0 and tq % (2 * span) == 0
    nq = length // tq
    hb = tq // span
    nhalo = length // span
    rot_v = a_rot.reshape(bsz, length, dil * wr)
    pln_v = a_plain.reshape(bsz, length, dil * wp)
    rb = wr // ATTN_WIDTH
    pb = wp // ATTN_WIDTH
    qcol, kcol, vcol = 2 * gi, 2 * gi + 1, 2 + gi

    def main(colbase, nb):
        return pl.BlockSpec((1, tq, ATTN_WIDTH), lambda b, r, i: (b, i, r * nb + colbase))

    def prev(colbase, nb):
        return pl.BlockSpec((1, span, ATTN_WIDTH),
                            lambda b, r, i: (b, jnp.maximum(i * hb - 1, 0), r * nb + colbase))

    def nxt(colbase, nb):
        return pl.BlockSpec((1, span, ATTN_WIDTH),
                            lambda b, r, i: (b, jnp.minimum((i + 1) * hb, nhalo - 1), r * nb + colbase))

    out_spec = pl.BlockSpec((1, tq, ATTN_WIDTH), lambda b, r, i: (b, i, r))
    o, lse = pl.pallas_call(
        functools.partial(_attn_kernel, tq=tq, span=span, length=length),
        grid=(bsz, dil, nq),
        in_specs=[main(qcol, rb), prev(kcol, rb), main(kcol, rb), nxt(kcol, rb),
                  prev(vcol, pb), main(vcol, pb), nxt(vcol, pb)],
        out_specs=[out_spec, out_spec],
        out_shape=[jax.ShapeDtypeStruct((bsz, length, dil * ATTN_WIDTH), BF16),
                   jax.ShapeDtypeStruct((bsz, length, dil * ATTN_WIDTH), F32)],
        compiler_params=_cparams(("parallel", "parallel", "parallel")),
        name=f"dilated_attn_{gi}",
    )(rot_v, rot_v, rot_v, rot_v, pln_v, pln_v, pln_v)
    return o.reshape(bsz, s, ATTN_WIDTH), lse.reshape(bsz, s, ATTN_WIDTH)


def _merge_kernel(x_ref, hg_ref, o0_ref, o1_ref, o2_ref, l0_ref, l1_ref, l2_ref, gate_ref,
                  who_ref, wao_ref, wout_ref, out_ref):
    d = x_ref.shape[1]
    yh = jnp.dot(hg_ref[...], who_ref[...], preferred_element_type=F32)
    la, lb, lc = l0_ref[...], l1_ref[...], l2_ref[...]
    mx = jnp.maximum(jnp.maximum(la, lb), lc)
    ea, eb, ec = jnp.exp(la - mx), jnp.exp(lb - mx), jnp.exp(lc - mx)
    num = ea * o0_ref[...].astype(F32) + eb * o1_ref[...].astype(F32) + ec * o2_ref[...].astype(F32)
    attn = (num * (1.0 / (ea + eb + ec))).astype(BF16)
    ya = jnp.dot(attn, wao_ref[...], preferred_element_type=F32)
    merged = gate_ref[:, :d].astype(F32) * yh + gate_ref[:, d:].astype(F32) * ya
    out_ref[...] = x_ref[...] + jnp.dot(merged.astype(BF16), wout_ref[...], preferred_element_type=F32)


def _merge(x2d, hg, outs, lses, gates, who, wao, wout, tm=512):
    m, d = x2d.shape
    tm = min(tm, m)
    row = lambda w: pl.BlockSpec((tm, w), lambda i: (i, 0))
    full = lambda a: pl.BlockSpec(a.shape, lambda i: (0, 0))
    return pl.pallas_call(
        _merge_kernel,
        grid=(m // tm,),
        in_specs=[row(d), row(d)] + [row(ATTN_WIDTH)] * 6 + [row(2 * d), full(who), full(wao), full(wout)],
        out_specs=row(d),
        out_shape=jax.ShapeDtypeStruct((m, d), F32),
        compiler_params=_cparams(("parallel",)),
        name="merge",
    )(x2d, hg, *outs, *lses, gates, who, wao, wout)


def _xattn_kernel(h_ref, g_ref, wq_ref, kv_ref, wo_ref, out_ref):
    hx = h_ref[0]
    d = hx.shape[1]
    hd = d // XA_HEADS
    nt = (((1,), (1,)), ((), ()))
    u = (hx * _rms_scale(hx) * g_ref[...]).astype(BF16)
    q = (jnp.dot(u, wq_ref[...], preferred_element_type=F32) * (hd ** -0.5)).astype(BF16)
    outs = []
    for h in range(XA_HEADS):
        kh = kv_ref[0, :, h * hd:(h + 1) * hd]
        vh = kv_ref[0, :, d + h * hd:d + (h + 1) * hd]
        s = lax.dot_general(q[:, h * hd:(h + 1) * hd], kh, nt, preferred_element_type=F32)
        p = jnp.exp(s - jnp.max(s, axis=-1, keepdims=True))
        den = jnp.sum(p, axis=-1, keepdims=True)
        outs.append(jnp.dot(p.astype(BF16), vh, preferred_element_type=F32) * (1.0 / den))
    o = jnp.concatenate(outs, axis=1).astype(BF16)
    out_ref[0] = hx + jnp.dot(o, wo_ref[...], preferred_element_type=F32)


def _xattn(h3d, g, wq, kv, wo, tm=512):
    bsz, s, d = h3d.shape
    tm = min(tm, s)
    full = lambda a: pl.BlockSpec(a.shape, lambda b, i: (0, 0))
    return pl.pallas_call(
        _xattn_kernel,
        grid=(bsz, s // tm),
        in_specs=[pl.BlockSpec((1, tm, d), lambda b, i: (b, i, 0)),
                  pl.BlockSpec((1, d), lambda b, i: (0, 0)),
                  full(wq),
                  pl.BlockSpec((1,) + kv.shape[1:], lambda b, i: (b, 0, 0)),
                  full(wo)],
        out_specs=pl.BlockSpec((1, tm, d), lambda b, i: (b, i, 0)),
        out_shape=jax.ShapeDtypeStruct((bsz, s, d), F32),
        compiler_params=_cparams(("parallel", "parallel")),
        name="xattn",
    )(h3d, g.reshape(1, d), wq, kv, wo)


def _mlp_kernel(h_ref, g_ref, w1_ref, w2_ref, gf_ref, out_ref):
    hx = h_ref[...]
    u = (hx * _rms_scale(hx) * g_ref[...]).astype(BF16)
    a = jnp.maximum(jnp.dot(u, w1_ref[...], preferred_element_type=F32), 0.0)
    a = (a * a).astype(BF16)
    y = hx + jnp.dot(a, w2_ref[...], preferred_element_type=F32)
    out_ref[...] = y * _rms_scale(y) * gf_ref[...]


def _mlp(h2d, g, w1, w2, gf, tm=512):
    m, d = h2d.shape
    tm = min(tm, m)
    full = lambda a: pl.BlockSpec(a.shape, lambda i: (0, 0))
    return pl.pallas_call(
        _mlp_kernel,
        grid=(m // tm,),
        in_specs=[pl.BlockSpec((tm, d), lambda i: (i, 0)), pl.BlockSpec((1, d), lambda i: (0, 0)),
                  full(w1), full(w2), pl.BlockSpec((1, d), lambda i: (0, 0))],
        out_specs=pl.BlockSpec((tm, d), lambda i: (i, 0)),
        out_shape=jax.ShapeDtypeStruct((m, d), F32),
        compiler_params=_cparams(("parallel",)),
        name="mlp_final",
    )(h2d, g.reshape(1, d), w1, w2, gf.reshape(1, d))


def _rotary_tables(s):
    half = ROT_DIM // 2
    inv = ROPE_THETA ** (-jnp.arange(half, dtype=F32) * 2.0 / ROT_DIM)
    ang = jnp.arange(s, dtype=F32)[:, None] * inv[None, :]
    cos, sin = jnp.cos(ang), jnp.sin(ang)
    pad = jnp.zeros((s, HEAD - ROT_DIM), F32)
    zero = jnp.zeros((s, half), F32)
    cos_t = jnp.concatenate([cos, cos, pad + 1.0], axis=1)
    sin_lo = jnp.concatenate([-sin, zero, pad], axis=1)
    sin_hi = jnp.concatenate([zero, sin, pad], axis=1)
    return cos_t, sin_lo, sin_hi


def _encode(x, mem, mix_norm_g, w_in, hgrn_lb_logits, hgrn_gnorm_g, w_hgrn_o, w_attn_o, w_out,
            xa_norm_g, mem_norm_g, w_xq, w_xkv, w_xo, ffn_norm_g, w_ffn1, w_ffn2, final_norm_g,
            hgrn_rows=2048):
    bsz, s, d = x.shape
    t = bsz * s
    depth = w_in.shape[0]
    fd = HG_HEADS * HEAD
    lb_all = jnp.cumsum(jax.nn.softmax(hgrn_lb_logits.astype(F32), axis=1), axis=1)
    sizes = (fd,) * 5 + (ATTN_WIDTH,) * 9 + (d, d)
    offs = np.concatenate([[0], np.cumsum(sizes)])
    cos_t, sin_lo, sin_hi = _rotary_tables(s)
    tm = min(512, s)
    tabspec = pl.BlockSpec((tm, HEAD), lambda i: (i % (s // tm), 0))
    h2d = x.reshape(t, d)
    for l in range(depth):
        wl = w_in[l]
        seg = lambda p: wl[:, offs[p]:offs[p + 1]]
        bf = lambda a: a.astype(BF16)
        w_silu = bf(jnp.concatenate([seg(0), seg(4)], axis=1))
        w_lf = bf(jnp.concatenate([seg(1), seg(2)], axis=1))
        w_plain = bf(jnp.concatenate([seg(3), seg(7), seg(10), seg(13)], axis=1))
        w_rot = bf(jnp.concatenate([seg(5), seg(6), seg(8), seg(9), seg(11), seg(12)], axis=1))
        w_gate = bf(jnp.concatenate([seg(14), seg(15)], axis=1))
        g_mix = mix_norm_g[l]

        a_silu = _norm_proj(h2d, g_mix, w_silu, _ep_silu, BF16, tm=tm, name="proj_silu")
        lb_row = lb_all[:, l].reshape(1, 2 * fd)
        a_lf = _norm_proj(h2d, g_mix, w_lf, _ep_logf, F32, extras=(lb_row,),
                          extra_specs=(pl.BlockSpec((1, 2 * fd), lambda i: (0, 0)),), tm=tm, name="proj_logf")
        a_plain = _norm_proj(h2d, g_mix, w_plain, _ep_plain, BF16, tm=tm, name="proj_plain")
        qscale = jnp.tile(jnp.concatenate([jnp.full((ATTN_WIDTH,), HEAD ** -0.5, F32),
                                           jnp.ones((ATTN_WIDTH,), F32)]), 3).reshape(1, 6 * ATTN_WIDTH)
        a_rot = _norm_proj(h2d, g_mix, w_rot, _ep_rotary, BF16,
                           extras=(cos_t, sin_lo, sin_hi, qscale),
                           extra_specs=(tabspec, tabspec, tabspec,
                                        pl.BlockSpec((1, 6 * ATTN_WIDTH), lambda i: (0, 0))),
                           tm=tm, name="proj_rotary")
        a_gate = _norm_proj(h2d, g_mix, w_gate, _ep_sigmoid, BF16, tm=tm, name="proj_gate")

        a_silu3 = a_silu.reshape(bsz, s, 2 * fd)
        a_lf3 = a_lf.reshape(bsz, s, 2 * fd)
        a_plain3 = a_plain.reshape(bsz, s, -1)
        o_fwd = _hgrn_pass(a_silu3, a_lf3, a_plain3, hgrn_gnorm_g[l], None, False, hgrn_rows)
        hg = _hgrn_pass(a_silu3, a_lf3, a_plain3, hgrn_gnorm_g[l], o_fwd, True, hgrn_rows)

        a_rot3 = a_rot.reshape(bsz, s, -1)
        outs, lses = [], []
        for gi, (win, dil) in enumerate(ATTN_GROUPS):
            o, lse = _dilated_attention(a_rot3, a_plain3, gi, (win // 2) // dil, dil)
            outs.append(o.reshape(t, ATTN_WIDTH))
            lses.append(lse.reshape(t, ATTN_WIDTH))

        h2d = _merge(h2d, hg.reshape(t, fd), outs, lses, a_gate,
                     w_hgrn_o[l].astype(BF16), w_attn_o[l].astype(BF16), w_out[l].astype(BF16), tm=tm)

        nm = mem.shape[1]
        kv = _norm_proj(mem.reshape(bsz * nm, d), mem_norm_g[l], w_xkv[l].astype(BF16), _ep_plain, BF16,
                        tm=nm, name="proj_memkv")
        h3d = _xattn(h2d.reshape(bsz, s, d), xa_norm_g[l], w_xq[l].astype(BF16), kv.reshape(bsz, nm, 2 * d),
                     w_xo[l].astype(BF16), tm=tm)
        h2d = h3d.reshape(t, d)
        last = l == depth - 1
        assert last, "final norm is fused into the last MLP call"
        h2d = _mlp(h2d, ffn_norm_g[l], w_ffn1[l].astype(BF16), w_ffn2[l].astype(BF16), final_norm_g, tm=tm)
    return h2d.reshape(bsz, s, d)


def kernel(x_prompt, x_sample, mem_prompt, mem_sample, mix_norm_g, w_in, hgrn_lb_logits, hgrn_gnorm_g,
           w_hgrn_o, w_attn_o, w_out, xa_norm_g, mem_norm_g, w_xq, w_xkv, w_xo, ffn_norm_g, w_ffn1,
           w_ffn2, final_norm_g):
    assert x_prompt.shape[1:] == x_sample.shape[1:]
    nb = x_prompt.shape[0]
    x = jnp.concatenate([x_prompt, x_sample], axis=0)
    mem = jnp.concatenate([mem_prompt, mem_sample], axis=0)
    y = _encode(x, mem, mix_norm_g, w_in, hgrn_lb_logits, hgrn_gnorm_g, w_hgrn_o, w_attn_o, w_out,
                xa_norm_g, mem_norm_g, w_xq, w_xkv, w_xo, ffn_norm_g, w_ffn1, w_ffn2, final_norm_g)
    return y[:nb], y[nb:]
```

```python
import functools

import numpy as np
import jax
import jax.numpy as jnp
from jax import lax
from jax.experimental import pallas as pl
from jax.experimental.pallas import tpu as pltpu

F32 = jnp.float32
BF16 = jnp.bfloat16

RMS_EPS = 1e-6
ROPE_THETA = 500000.0
HG_HEADS = 8
HEAD = 128
ATTN_HEADS = 4
ATTN_WIDTH = ATTN_HEADS * HEAD
ATTN_GROUPS = ((128, 1), (512, 4), (2048, 16))
ROT_DIM = HEAD // 4
XA_HEADS = 4
CHUNK = 64
SUB = 8
NEG = -1e30
HG_PACK = 2
SAFE_CHUNK_LOG_DECAY = -60.0
HGRN_UNROLL = 4
ATTN_UNROLL = 4
NT_DIMS = (((1,), (1,)), ((), ()))
TN_DIMS = (((0,), (0,)), ((), ()))

VMEM_LIMIT = 56 * 1024 * 1024


def _cparams(sem):
    return pltpu.CompilerParams(dimension_semantics=sem, vmem_limit_bytes=VMEM_LIMIT)


def _sigmoid(x):
    return 1.0 / (1.0 + jnp.exp(-x))


def _rms_scale(xf):
    return lax.rsqrt(jnp.mean(xf * xf, axis=-1, keepdims=True) + RMS_EPS)


def _ep_plain(acc):
    return acc


def _ep_silu(acc):
    return acc * _sigmoid(acc)


def _ep_sigmoid(acc):
    return _sigmoid(acc)


def _ep_logf(acc, lb):
    return jnp.log(lb + (1.0 - lb) * _sigmoid(acc))


def _ep_rotary(acc, cos, sin_lo, sin_hi, colscale):
    half = ROT_DIM // 2
    outs = []
    for c in range(acc.shape[1] // HEAD):
        t = acc[:, c * HEAD:(c + 1) * HEAD]
        r = t * cos + pltpu.roll(t, HEAD - half, 1) * sin_lo + pltpu.roll(t, half, 1) * sin_hi
        outs.append(r)
    return jnp.concatenate(outs, axis=1) * colscale


def _norm_proj_kernel(*refs, epilogue, n_extra):
    x_ref, g_ref, w_ref = refs[:3]
    extra = refs[3:3 + n_extra]
    o_ref = refs[3 + n_extra]
    xf = x_ref[...]
    u = (xf * _rms_scale(xf) * g_ref[...]).astype(BF16)
    acc = jnp.dot(u, w_ref[...], preferred_element_type=F32)
    o_ref[...] = epilogue(acc, *[e[...] for e in extra]).astype(o_ref.dtype)


def _norm_proj(x2d, g, w, epilogue, out_dtype, extras=(), extra_specs=(), tm=512, name="norm_proj"):
    m, d = x2d.shape
    n = w.shape[1]
    tm = min(tm, m)
    assert m % tm == 0
    in_specs = [
        pl.BlockSpec((tm, d), lambda i: (i, 0)),
        pl.BlockSpec((1, d), lambda i: (0, 0)),
        pl.BlockSpec((d, n), lambda i: (0, 0)),
    ] + list(extra_specs)
    return pl.pallas_call(
        functools.partial(_norm_proj_kernel, epilogue=epilogue, n_extra=len(extras)),
        grid=(m // tm,),
        in_specs=in_specs,
        out_specs=pl.BlockSpec((tm, n), lambda i: (i, 0)),
        out_shape=jax.ShapeDtypeStruct((m, n), out_dtype),
        compiler_params=_cparams(("parallel",)),
        name=name,
    )(x2d, g.reshape(1, d), w, *extras)


def _hgrn_robust_chunk(r0, ls, q_ref, lf_ref, v_ref, tri_ref, st_ref, kc_ref, bc_ref, rev):
    C = CHUNK
    ng = C // SUB
    rowl = lax.broadcasted_iota(jnp.int32, (C, HEAD), 0)
    sub_pos = rowl & (SUB - 1)
    arow = lax.broadcasted_iota(jnp.int32, (C, C), 0)
    acol = lax.broadcasted_iota(jnp.int32, (C, C), 1)
    ones_b = jnp.ones((HEAD, HEAD), BF16)
    zeros_g = jnp.zeros((SUB, HEAD), F32)

    lf = lf_ref[0, pl.ds(r0, C), ls]
    q = q_ref[0, pl.ds(r0, C), ls].astype(F32)
    v = v_ref[0, pl.ds(r0, C), ls]
    k = 1.0 - jnp.exp(lf)
    hi = lf.astype(BF16)
    mid = (lf - hi.astype(F32)).astype(BF16)
    b = jnp.dot(tri_ref[...], jnp.concatenate([hi, mid], axis=0), preferred_element_type=F32)
    kc_ref[pl.ds(SUB, C), :] = k
    bc_ref[pl.ds(SUB, C), :] = b
    b_end = bc_ref[pl.ds(SUB + (0 if rev else C - 1), 1), :]

    st = st_ref[:, ls]
    qi = (q * jnp.exp(b)).astype(BF16)
    o = lax.dot_general(qi, st.astype(BF16), NT_DIMS, preferred_element_type=F32)
    kl = (k * jnp.exp(b_end - b)).astype(BF16)
    st_ref[:, ls] = st * jnp.exp(b_end) + lax.dot_general(v, kl, TN_DIMS, preferred_element_type=F32)

    a = jnp.zeros((C, C), F32)
    h = C // 2
    while h >= SUB:
        qparts, kparts = [], []
        for gi in range(ng):
            t0 = gi * SUB
            blk = t0 // (2 * h)
            in_upper = (t0 % (2 * h)) >= h
            is_query = (not in_upper) if rev else in_upper
            rr = blk * 2 * h + (h if rev else h - 1)
            bref = bc_ref[pl.ds(SUB + rr, 1), :]
            bg = b[t0:t0 + SUB]
            if is_query:
                qparts.append(q[t0:t0 + SUB] * jnp.exp(bg - bref))
                kparts.append(zeros_g)
            else:
                qparts.append(zeros_g)
                kparts.append(k[t0:t0 + SUB] * jnp.exp(bref - bg))
        qh = jnp.concatenate(qparts, axis=0).astype(BF16)
        kh = jnp.concatenate(kparts, axis=0).astype(BF16)
        ah = lax.dot_general(qh, kh, NT_DIMS, preferred_element_type=F32)
        if 2 * h < C:
            ah = jnp.where((arow ^ acol) < 2 * h, ah, 0.0)
        a = a + ah
        h //= 2

    for d in range(SUB):
        sh = SUB + (d if rev else -d)
        ks = kc_ref[pl.ds(sh, C), :]
        bs = bc_ref[pl.ds(sh, C), :]
        ok = (sub_pos + d <= SUB - 1) if rev else (sub_pos >= d)
        p = jnp.where(ok, q * ks * jnp.exp(b - bs), 0.0).astype(BF16)
        rs = jnp.dot(p, ones_b, preferred_element_type=F32)[:, :C]
        tgt = (arow + d) if rev else (arow - d)
        a = a + jnp.where(acol == tgt, rs, 0.0)

    return o + jnp.dot(a.astype(BF16), v, preferred_element_type=F32)


def _blockdiag2(x):
    z = jnp.zeros((x.shape[0], HEAD), x.dtype)
    return jnp.concatenate([jnp.concatenate([x[:, :HEAD], z], axis=1),
                            jnp.concatenate([z, x[:, HEAD:]], axis=1)], axis=0)


def _hgrn_kernel(*refs, rev, final, nchunk):
    if final:
        (q_ref, lf_ref, v_ref, tri_ref, sel_ref, of_ref, g_ref, gn_ref, o_ref,
         st_ref, qi_ref, a_ref, upd_ref, snap_ref, dec_ref, kc_ref, bc_ref) = refs
    else:
        (q_ref, lf_ref, v_ref, tri_ref, sel_ref, o_ref,
         st_ref, qi_ref, a_ref, upd_ref, snap_ref, dec_ref, kc_ref, bc_ref) = refs
        of_ref = g_ref = gn_ref = None
    C = CHUNK
    W = HG_PACK * HEAD

    @pl.when(pl.program_id(2) == 0)
    def _():
        st_ref[...] = jnp.zeros_like(st_ref)
        kc_ref[...] = jnp.zeros_like(kc_ref)
        bc_ref[...] = jnp.zeros_like(bc_ref)

    def emit(o, r0, ls):
        if final:
            tot = o + of_ref[0, pl.ds(r0, C), ls]
            parts = []
            for j in range(tot.shape[1] // HEAD):
                tj = tot[:, j * HEAD:(j + 1) * HEAD]
                parts.append(tj * _rms_scale(tj) * gn_ref[...])
            y = parts[0] if len(parts) == 1 else jnp.concatenate(parts, axis=1)
            y = y * g_ref[0, pl.ds(r0, C), ls].astype(F32)
            o_ref[0, pl.ds(r0, C), ls] = y.astype(o_ref.dtype)
        else:
            o_ref[0, pl.ds(r0, C), ls] = o

    sums = jnp.dot(sel_ref[...], lf_ref[0].astype(BF16), preferred_element_type=F32)
    safe = jnp.min(sums) >= SAFE_CHUNK_LOG_DECAY

    @pl.when(safe)
    def _():
        arow = lax.broadcasted_iota(jnp.int32, (C, 2 * C), 0)
        acol = lax.broadcasted_iota(jnp.int32, (C, 2 * C), 1) & (C - 1)
        causal = (acol >= arow) if rev else (acol <= arow)

        G = HGRN_UNROLL
        assert nchunk % G == 0

        def phase1(i, carry):
            cs = [i * G + g for g in range(G)]
            r0s = [pl.multiple_of(c * C, C) for c in cs]
            lfs = [lf_ref[0, pl.ds(r0, C), :] for r0 in r0s]
            his = [lf.astype(BF16) for lf in lfs]
            mids = [(lf - hi.astype(F32)).astype(BF16) for lf, hi in zip(lfs, his)]
            rhs = jnp.concatenate([jnp.concatenate(his, axis=1), jnp.concatenate(mids, axis=1)], axis=0)
            ball = jnp.dot(tri_ref[...], rhs, preferred_element_type=F32)
            qis, kls, kbars = [], [], []
            for g in range(G):
                b = ball[:, g * W:(g + 1) * W]
                k = 1.0 - jnp.exp(lfs[g])
                q = q_ref[0, pl.ds(r0s[g], C), :].astype(F32)
                qi = (q * jnp.exp(b)).astype(BF16)
                qi_ref[pl.ds(r0s[g], C), :] = qi
                b_end = b[0:1] if rev else b[C - 1:C]
                dec_ref[pl.ds(pl.multiple_of(cs[g] * SUB, SUB), SUB), :] = jnp.broadcast_to(jnp.exp(b_end), (SUB, W))
                qis.append(qi)
                kls.append((k * jnp.exp(b_end - b)).astype(BF16))
                kbars.append((k * jnp.exp(-b)).astype(BF16))
            avals = [lax.dot_general(qis[g], _blockdiag2(kbars[g]), NT_DIMS, preferred_element_type=F32)
                     for g in range(G)]
            for g in range(G):
                v = v_ref[0, pl.ds(r0s[g], C), :]
                for j in range(HG_PACK):
                    ls = slice(j * HEAD, (j + 1) * HEAD)
                    upd_ref[pl.ds(pl.multiple_of(cs[g] * HEAD, HEAD), HEAD), ls] = lax.dot_general(
                        v[:, ls], kls[g][:, ls], TN_DIMS, preferred_element_type=F32)
            for g in range(G):
                a_ref[pl.ds(r0s[g], C), :] = jnp.where(causal, avals[g], 0.0).astype(BF16)
            return carry

        lax.fori_loop(0, nchunk // G, phase1, 0)

        def phase2(i, carry):
            c = (nchunk - 1 - i) if rev else i
            s0 = pl.multiple_of(c * HEAD, HEAD)
            st = st_ref[...]
            snap_ref[pl.ds(s0, HEAD), :] = st.astype(BF16)
            dec = dec_ref[pl.ds(pl.multiple_of(c * SUB, SUB), 1), :]
            st_ref[...] = st * dec + upd_ref[pl.ds(s0, HEAD), :]
            return carry

        lax.fori_loop(0, nchunk, phase2, 0, unroll=G)

        def phase3(i, carry):
            cs = [i * G + g for g in range(G)]
            r0s = [pl.multiple_of(c * C, C) for c in cs]
            snaps = [snap_ref[pl.ds(pl.multiple_of(c * HEAD, HEAD), HEAD), :] for c in cs]
            o1 = [lax.dot_general(qi_ref[pl.ds(r0s[g], C), :], _blockdiag2(snaps[g]), NT_DIMS,
                                  preferred_element_type=F32) for g in range(G)]
            o2 = [jnp.dot(a_ref[pl.ds(r0s[g], C), :], _blockdiag2(v_ref[0, pl.ds(r0s[g], C), :]),
                          preferred_element_type=F32) for g in range(G)]
            for g in range(G):
                emit(o1[g] + o2[g], r0s[g], slice(None))
            return carry

        lax.fori_loop(0, nchunk // G, phase3, 0)

    @pl.when(jnp.logical_not(safe))
    def _():
        def body(i, carry):
            c = (nchunk - 1 - i) if rev else i
            r0 = pl.multiple_of(c * C, C)
            for j in range(HG_PACK):
                ls = slice(j * HEAD, (j + 1) * HEAD)
                o = _hgrn_robust_chunk(r0, ls, q_ref, lf_ref, v_ref, tri_ref, st_ref, kc_ref, bc_ref, rev)
                emit(o, r0, ls)
            return carry

        lax.fori_loop(0, nchunk, body, 0)


def _tri_matrix(rev):
    t = np.arange(CHUNK)
    m = (t[None, :] >= t[:, None]) if rev else (t[None, :] <= t[:, None])
    return jnp.asarray(np.concatenate([m, m], axis=1).astype(np.float32), dtype=BF16)


def _chunk_sum_matrix(rows):
    nchunk = rows // CHUNK
    nsel = -(-nchunk // 16) * 16
    m = np.zeros((nsel, rows), np.float32)
    m[np.arange(rows) // CHUNK, np.arange(rows)] = 1.0
    return jnp.asarray(m, dtype=BF16)


def _hgrn_pass(a_silu, a_lf, a_plain, gnorm, o_fwd, rev, rows):
    bsz, s, _ = a_silu.shape
    rows = min(rows, s)
    nblk = s // rows
    nchunk = rows // CHUNK
    w = HG_PACK * HEAD
    npk = HG_HEADS // HG_PACK
    final = o_fwd is not None
    seq = (lambda i: nblk - 1 - i) if rev else (lambda i: i)
    fcol = npk if rev else 0
    sel = _chunk_sum_matrix(rows)
    in_specs = [
        pl.BlockSpec((1, rows, w), lambda b, h, i: (b, seq(i), h)),
        pl.BlockSpec((1, rows, w), lambda b, h, i: (b, seq(i), h + fcol)),
        pl.BlockSpec((1, rows, w), lambda b, h, i: (b, seq(i), h)),
        pl.BlockSpec((CHUNK, 2 * CHUNK), lambda b, h, i: (0, 0)),
        pl.BlockSpec(sel.shape, lambda b, h, i: (0, 0)),
    ]
    args = [a_silu, a_lf, a_plain, _tri_matrix(rev), sel]
    if final:
        in_specs += [
            pl.BlockSpec((1, rows, w), lambda b, h, i: (b, seq(i), h)),
            pl.BlockSpec((1, rows, w), lambda b, h, i: (b, seq(i), h + npk)),
            pl.BlockSpec((1, HEAD), lambda b, h, i: (0, 0)),
        ]
        args += [o_fwd, a_silu, gnorm.reshape(1, HEAD)]
    return pl.pallas_call(
        functools.partial(_hgrn_kernel, rev=rev, final=final, nchunk=nchunk),
        grid=(bsz, npk, nblk),
        in_specs=in_specs,
        out_specs=pl.BlockSpec((1, rows, w), lambda b, h, i: (b, seq(i), h)),
        out_shape=jax.ShapeDtypeStruct((bsz, s, HG_HEADS * HEAD), BF16 if final else F32),
        scratch_shapes=[
            pltpu.VMEM((HEAD, w), F32),
            pltpu.VMEM((rows, w), BF16),
            pltpu.VMEM((rows, 2 * CHUNK), BF16),
            pltpu.VMEM((nchunk * HEAD, w), F32),
            pltpu.VMEM((nchunk * HEAD, w), BF16),
            pltpu.VMEM((nchunk * SUB, w), F32),
            pltpu.VMEM((CHUNK + 2 * SUB, HEAD), F32),
            pltpu.VMEM((CHUNK + 2 * SUB, HEAD), F32),
        ],
        compiler_params=_cparams(("parallel", "parallel", "arbitrary")),
        name="hgrn_bwd" if rev else "hgrn_fwd",
    )(*args)


def _attn_group(q_ref, k_ref, v_ref, qs_ref, ks_ref, vs_ref, acc_ref, m_ref, l_ref, *, dil, span, first):
    s = q_ref.shape[1]
    length = s // dil
    qt, kt = 2 * span, 4 * span
    ntile = length // qt
    dcol = (lax.broadcasted_iota(jnp.int32, (qt, kt), 1) - lax.broadcasted_iota(jnp.int32, (qt, kt), 0))
    if dil > 1:
        qs_ref[...] = q_ref[0].astype(F32)
        ks_ref[...] = k_ref[0].astype(F32)
        vs_ref[...] = v_ref[0].astype(F32)

    for r in range(dil):
        def tile(j, carry):
            w0 = jnp.clip(j * qt - span, 0, length - kt)
            qbase = pl.multiple_of(j * (qt * dil), qt * dil)
            kbase = pl.multiple_of(w0 * dil, span * dil)
            if dil > 1:
                rq = pl.ds(r, qt, stride=dil)
                rk = pl.ds(r, kt, stride=dil)
                q = qs_ref.at[pl.ds(qbase, qt * dil)][rq, :].astype(BF16)
                kw = ks_ref.at[pl.ds(kbase, kt * dil)][rk, :].astype(BF16)
                vw = vs_ref.at[pl.ds(kbase, kt * dil)][rk, :].astype(BF16)
            else:
                q = q_ref[0, pl.ds(qbase, qt), :]
                kw = k_ref[0, pl.ds(kbase, kt), :]
                vw = v_ref[0, pl.ds(kbase, kt), :]
            sc = lax.dot_general(q, kw, NT_DIMS, preferred_element_type=F32)
            sc = jnp.where(jnp.abs(dcol + (w0 - j * qt)) <= span, sc, NEG)
            mx = jnp.max(sc, axis=-1, keepdims=True)
            p = jnp.exp(sc - mx)
            ln = jnp.broadcast_to(jnp.sum(p, axis=-1, keepdims=True), (qt, HEAD))
            on = jnp.dot(p.astype(BF16), vw, preferred_element_type=F32)
            mn = jnp.broadcast_to(mx, (qt, HEAD))
            mv = m_ref.at[pl.ds(qbase, qt * dil)]
            lv = l_ref.at[pl.ds(qbase, qt * dil)]
            av = acc_ref.at[pl.ds(qbase, qt * dil)]
            rows = pl.ds(r, qt, stride=dil) if dil > 1 else slice(None)
            if first:
                mv[rows, :] = mn
                lv[rows, :] = ln
                av[rows, :] = on
            else:
                mo = mv[rows, :]
                mm = jnp.maximum(mo, mn)
                wo = jnp.exp(mo - mm)
                wn = jnp.exp(mn - mm)
                mv[rows, :] = mm
                lv[rows, :] = wo * lv[rows, :] + wn * ln
                av[rows, :] = wo * av[rows, :] + wn * on
            return carry

        lax.fori_loop(0, ntile, tile, 0, unroll=min(ATTN_UNROLL, ntile))


def _attn_kernel(q_ref, k_ref, v_ref, o_ref, qs_ref, ks_ref, vs_ref, acc_ref, m_ref, l_ref):
    g = pl.program_id(2)
    for gi, (win, dil) in enumerate(ATTN_GROUPS):
        @pl.when(g == gi)
        def _(gi=gi, win=win, dil=dil):
            _attn_group(q_ref, k_ref, v_ref, qs_ref, ks_ref, vs_ref, acc_ref, m_ref, l_ref,
                        dil=dil, span=(win // 2) // dil, first=(gi == 0))

    @pl.when(g == len(ATTN_GROUPS) - 1)
    def _():
        o_ref[0] = (acc_ref[...] * (1.0 / l_ref[...])).astype(o_ref.dtype)


def _dilated_attention(a_rot, a_plain):
    bsz, s, _ = a_rot.shape
    ng = len(ATTN_GROUPS)
    vbase = HG_HEADS
    blk = (1, s, HEAD)
    return pl.pallas_call(
        _attn_kernel,
        grid=(bsz, ATTN_HEADS, ng),
        in_specs=[pl.BlockSpec(blk, lambda b, h, g: (b, 0, 2 * g * ATTN_HEADS + h)),
                  pl.BlockSpec(blk, lambda b, h, g: (b, 0, (2 * g + 1) * ATTN_HEADS + h)),
                  pl.BlockSpec(blk, lambda b, h, g: (b, 0, vbase + g * ATTN_HEADS + h))],
        out_specs=pl.BlockSpec(blk, lambda b, h, g: (b, 0, h)),
        out_shape=jax.ShapeDtypeStruct((bsz, s, ATTN_WIDTH), BF16),
        scratch_shapes=[pltpu.VMEM((s, HEAD), F32)] * 6,
        compiler_params=_cparams(("parallel", "parallel", "arbitrary")),
        name="dilated_attn",
    )(a_rot, a_rot, a_plain)


def _merge_kernel(x_ref, hg_ref, attn_ref, gate_ref, who_ref, wao_ref, wout_ref, out_ref):
    d = x_ref.shape[1]
    yh = jnp.dot(hg_ref[...], who_ref[...], preferred_element_type=F32)
    ya = jnp.dot(attn_ref[...], wao_ref[...], preferred_element_type=F32)
    merged = gate_ref[:, :d].astype(F32) * yh + gate_ref[:, d:].astype(F32) * ya
    out_ref[...] = x_ref[...] + jnp.dot(merged.astype(BF16), wout_ref[...], preferred_element_type=F32)


def _merge(x2d, hg, attn, gates, who, wao, wout, tm=512):
    m, d = x2d.shape
    tm = min(tm, m)
    row = lambda w: pl.BlockSpec((tm, w), lambda i: (i, 0))
    full = lambda a: pl.BlockSpec(a.shape, lambda i: (0, 0))
    return pl.pallas_call(
        _merge_kernel,
        grid=(m // tm,),
        in_specs=[row(d), row(d), row(ATTN_WIDTH), row(2 * d), full(who), full(wao), full(wout)],
        out_specs=row(d),
        out_shape=jax.ShapeDtypeStruct((m, d), F32),
        compiler_params=_cparams(("parallel",)),
        name="merge",
    )(x2d, hg, attn, gates, who, wao, wout)


def _xattn_kernel(h_ref, g_ref, wq_ref, kv_ref, wo_ref, out_ref):
    hx = h_ref[0]
    d = hx.shape[1]
    hd = d // XA_HEADS
    u = (hx * _rms_scale(hx) * g_ref[...]).astype(BF16)
    q = (jnp.dot(u, wq_ref[...], preferred_element_type=F32) * (hd ** -0.5)).astype(BF16)
    outs = []
    for h in range(XA_HEADS):
        kh = kv_ref[0, :, h * hd:(h + 1) * hd]
        vh = kv_ref[0, :, d + h * hd:d + (h + 1) * hd]
        s = lax.dot_general(q[:, h * hd:(h + 1) * hd], kh, NT_DIMS, preferred_element_type=F32)
        p = jnp.exp(s - jnp.max(s, axis=-1, keepdims=True))
        den = jnp.sum(p, axis=-1, keepdims=True)
        outs.append(jnp.dot(p.astype(BF16), vh, preferred_element_type=F32) * (1.0 / den))
    o = jnp.concatenate(outs, axis=1).astype(BF16)
    out_ref[0] = hx + jnp.dot(o, wo_ref[...], preferred_element_type=F32)


def _xattn(h3d, g, wq, kv, wo, tm=512):
    bsz, s, d = h3d.shape
    tm = min(tm, s)
    full = lambda a: pl.BlockSpec(a.shape, lambda b, i: (0, 0))
    return pl.pallas_call(
        _xattn_kernel,
        grid=(bsz, s // tm),
        in_specs=[pl.BlockSpec((1, tm, d), lambda b, i: (b, i, 0)),
                  pl.BlockSpec((1, d), lambda b, i: (0, 0)),
                  full(wq),
                  pl.BlockSpec((1,) + kv.shape[1:], lambda b, i: (b, 0, 0)),
                  full(wo)],
        out_specs=pl.BlockSpec((1, tm, d), lambda b, i: (b, i, 0)),
        out_shape=jax.ShapeDtypeStruct((bsz, s, d), F32),
        compiler_params=_cparams(("parallel", "parallel")),
        name="xattn",
    )(h3d, g.reshape(1, d), wq, kv, wo)


def _mlp_kernel(h_ref, g_ref, w1_ref, w2_ref, gf_ref, out_ref):
    hx = h_ref[...]
    u = (hx * _rms_scale(hx) * g_ref[...]).astype(BF16)
    a = jnp.maximum(jnp.dot(u, w1_ref[...], preferred_element_type=F32), 0.0)
    a = (a * a).astype(BF16)
    y = hx + jnp.dot(a, w2_ref[...], preferred_element_type=F32)
    out_ref[...] = y * _rms_scale(y) * gf_ref[...]


def _mlp(h2d, g, w1, w2, gf, tm=512):
    m, d = h2d.shape
    tm = min(tm, m)
    full = lambda a: pl.BlockSpec(a.shape, lambda i: (0, 0))
    return pl.pallas_call(
        _mlp_kernel,
        grid=(m // tm,),
        in_specs=[pl.BlockSpec((tm, d), lambda i: (i, 0)), pl.BlockSpec((1, d), lambda i: (0, 0)),
                  full(w1), full(w2), pl.BlockSpec((1, d), lambda i: (0, 0))],
        out_specs=pl.BlockSpec((tm, d), lambda i: (i, 0)),
        out_shape=jax.ShapeDtypeStruct((m, d), F32),
        compiler_params=_cparams(("parallel",)),
        name="mlp_final",
    )(h2d, g.reshape(1, d), w1, w2, gf.reshape(1, d))


def _rotary_tables(s):
    half = ROT_DIM // 2
    inv = ROPE_THETA ** (-jnp.arange(half, dtype=F32) * 2.0 / ROT_DIM)
    ang = jnp.arange(s, dtype=F32)[:, None] * inv[None, :]
    cos, sin = jnp.cos(ang), jnp.sin(ang)
    pad = jnp.zeros((s, HEAD - ROT_DIM), F32)
    zero = jnp.zeros((s, half), F32)
    cos_t = jnp.concatenate([cos, cos, pad + 1.0], axis=1)
    sin_lo = jnp.concatenate([-sin, zero, pad], axis=1)
    sin_hi = jnp.concatenate([zero, sin, pad], axis=1)
    return cos_t, sin_lo, sin_hi


def _encode(x, mem, mix_norm_g, w_in, hgrn_lb_logits, hgrn_gnorm_g, w_hgrn_o, w_attn_o, w_out,
            xa_norm_g, mem_norm_g, w_xq, w_xkv, w_xo, ffn_norm_g, w_ffn1, w_ffn2, final_norm_g,
            hgrn_rows=2048):
    bsz, s, d = x.shape
    t = bsz * s
    depth = w_in.shape[0]
    fd = HG_HEADS * HEAD
    lb_all = jnp.cumsum(jax.nn.softmax(hgrn_lb_logits.astype(F32), axis=1), axis=1)
    sizes = (fd,) * 5 + (ATTN_WIDTH,) * 9 + (d, d)
    offs = np.concatenate([[0], np.cumsum(sizes)])
    cos_t, sin_lo, sin_hi = _rotary_tables(s)
    tm = min(512, s)
    tabspec = pl.BlockSpec((tm, HEAD), lambda i: (i % (s // tm), 0))
    h2d = x.reshape(t, d)
    for l in range(depth):
        wl = w_in[l]
        seg = lambda p: wl[:, offs[p]:offs[p + 1]]
        bf = lambda a: a.astype(BF16)
        w_silu = bf(jnp.concatenate([seg(0), seg(4)], axis=1))
        w_lf = bf(jnp.concatenate([seg(1), seg(2)], axis=1))
        w_plain = bf(jnp.concatenate([seg(3), seg(7), seg(10), seg(13)], axis=1))
        w_rot = bf(jnp.concatenate([seg(5), seg(6), seg(8), seg(9), seg(11), seg(12)], axis=1))
        w_gate = bf(jnp.concatenate([seg(14), seg(15)], axis=1))
        g_mix = mix_norm_g[l]

        a_silu = _norm_proj(h2d, g_mix, w_silu, _ep_silu, BF16, tm=tm, name="proj_silu")
        lb_row = lb_all[:, l].reshape(1, 2 * fd)
        a_lf = _norm_proj(h2d, g_mix, w_lf, _ep_logf, F32, extras=(lb_row,),
                          extra_specs=(pl.BlockSpec((1, 2 * fd), lambda i: (0, 0)),), tm=tm, name="proj_logf")
        a_plain = _norm_proj(h2d, g_mix, w_plain, _ep_plain, BF16, tm=tm, name="proj_plain")
        qscale = jnp.tile(jnp.concatenate([jnp.full((ATTN_WIDTH,), HEAD ** -0.5, F32),
                                           jnp.ones((ATTN_WIDTH,), F32)]), 3).reshape(1, 6 * ATTN_WIDTH)
        a_rot = _norm_proj(h2d, g_mix, w_rot, _ep_rotary, BF16,
                           extras=(cos_t, sin_lo, sin_hi, qscale),
                           extra_specs=(tabspec, tabspec, tabspec,
                                        pl.BlockSpec((1, 6 * ATTN_WIDTH), lambda i: (0, 0))),
                           tm=tm, name="proj_rotary")
        a_gate = _norm_proj(h2d, g_mix, w_gate, _ep_sigmoid, BF16, tm=tm, name="proj_gate")

        a_silu3 = a_silu.reshape(bsz, s, 2 * fd)
        a_lf3 = a_lf.reshape(bsz, s, 2 * fd)
        a_plain3 = a_plain.reshape(bsz, s, -1)
        o_fwd = _hgrn_pass(a_silu3, a_lf3, a_plain3, hgrn_gnorm_g[l], None, False, hgrn_rows)
        hg = _hgrn_pass(a_silu3, a_lf3, a_plain3, hgrn_gnorm_g[l], o_fwd, True, hgrn_rows)

        attn = _dilated_attention(a_rot.reshape(bsz, s, -1), a_plain3)

        h2d = _merge(h2d, hg.reshape(t, fd), attn.reshape(t, ATTN_WIDTH), a_gate,
                     w_hgrn_o[l].astype(BF16), w_attn_o[l].astype(BF16), w_out[l].astype(BF16), tm=tm)

        nm = mem.shape[1]
        kv = _norm_proj(mem.reshape(bsz * nm, d), mem_norm_g[l], w_xkv[l].astype(BF16), _ep_plain, BF16,
                        tm=nm, name="proj_memkv")
        h3d = _xattn(h2d.reshape(bsz, s, d), xa_norm_g[l], w_xq[l].astype(BF16), kv.reshape(bsz, nm, 2 * d),
                     w_xo[l].astype(BF16), tm=tm)
        h2d = h3d.reshape(t, d)
        assert l == depth - 1, "final norm is fused into the last MLP call"
        h2d = _mlp(h2d, ffn_norm_g[l], w_ffn1[l].astype(BF16), w_ffn2[l].astype(BF16), final_norm_g, tm=tm)
    return h2d.reshape(bsz, s, d)


def kernel(x_prompt, x_sample, mem_prompt, mem_sample, mix_norm_g, w_in, hgrn_lb_logits, hgrn_gnorm_g,
           w_hgrn_o, w_attn_o, w_out, xa_norm_g, mem_norm_g, w_xq, w_xkv, w_xo, ffn_norm_g, w_ffn1,
           w_ffn2, final_norm_g):
    assert x_prompt.shape[1:] == x_sample.shape[1:]
    nb = x_prompt.shape[0]
    x = jnp.concatenate([x_prompt, x_sample], axis=0)
    mem = jnp.concatenate([mem_prompt, mem_sample], axis=0)
    y = _encode(x, mem, mix_norm_g, w_in, hgrn_lb_logits, hgrn_gnorm_g, w_hgrn_o, w_attn_o, w_out,
                xa_norm_g, mem_norm_g, w_xq, w_xkv, w_xo, ffn_norm_g, w_ffn1, w_ffn2, final_norm_g)
    return y[:nb], y[nb:]
```

```python
import functools

import numpy as np
import jax
import jax.numpy as jnp
from jax import lax
from jax.experimental import pallas as pl
from jax.experimental.pallas import tpu as pltpu

F32 = jnp.float32
BF16 = jnp.bfloat16

RMS_EPS = 1e-6
ROPE_THETA = 500000.0
HG_HEADS = 8
HEAD = 128
ATTN_HEADS = 4
ATTN_WIDTH = ATTN_HEADS * HEAD
ATTN_GROUPS = ((128, 1), (512, 4), (2048, 16))
ROT_DIM = HEAD // 4
XA_HEADS = 4
CHUNK = 64
SUB = 8
NEG = -1e30
HG_PACK = 2
SAFE_CHUNK_LOG_DECAY = -60.0
HGRN_UNROLL = 4
ATTN_UNROLL = 4
NT_DIMS = (((1,), (1,)), ((), ()))
TN_DIMS = (((0,), (0,)), ((), ()))

TOKEN_TILE = 512
PROJ_TOKEN_TILE = 1024

VMEM_LIMIT = 56 * 1024 * 1024


def _cparams(sem):
    return pltpu.CompilerParams(dimension_semantics=sem, vmem_limit_bytes=VMEM_LIMIT)


def _sigmoid(x):
    return 1.0 / (1.0 + jnp.exp(-x))


def _rms_scale(xf):
    return lax.rsqrt(jnp.mean(xf * xf, axis=-1, keepdims=True) + RMS_EPS)


def _ep_plain(acc):
    return acc


def _ep_silu(acc):
    return acc * _sigmoid(acc)


def _ep_sigmoid(acc):
    return _sigmoid(acc)


def _ep_logf(acc, lb):
    return jnp.log(lb + (1.0 - lb) * _sigmoid(acc))


def _ep_rotary(acc, cos, sin_lo, sin_hi, colscale):
    half = ROT_DIM // 2
    outs = []
    for c in range(acc.shape[1] // HEAD):
        t = acc[:, c * HEAD:(c + 1) * HEAD]
        r = t * cos + pltpu.roll(t, HEAD - half, 1) * sin_lo + pltpu.roll(t, half, 1) * sin_hi
        outs.append(r)
    return jnp.concatenate(outs, axis=1) * colscale


def _row_sources(srcs, tm):
    bounds = [0]
    for a in srcs:
        assert a.shape[0] % tm == 0
        bounds.append(bounds[-1] + a.shape[0] // tm)
    specs = [pl.BlockSpec((tm, a.shape[1]),
                          lambda i, lo=bounds[k], n=bounds[k + 1] - bounds[k]: (jnp.clip(i - lo, 0, n - 1), 0))
             for k, a in enumerate(srcs)]
    return specs, bounds


def _pick_source(i, refs, bounds):
    x = refs[-1][...]
    for k in range(len(refs) - 2, -1, -1):
        x = jnp.where(i < bounds[k + 1], refs[k][...], x)
    return x


def _prenorm_kernel(*refs, bounds):
    nsrc = len(bounds) - 1
    g_ref, o_ref = refs[nsrc], refs[nsrc + 1]
    xf = _pick_source(pl.program_id(0), refs[:nsrc], bounds)
    o_ref[...] = (xf * _rms_scale(xf) * g_ref[...]).astype(o_ref.dtype)


def _prenorm(srcs, g, tm):
    d = srcs[0].shape[1]
    specs, bounds = _row_sources(srcs, tm)
    return pl.pallas_call(
        functools.partial(_prenorm_kernel, bounds=bounds),
        grid=(bounds[-1],),
        in_specs=specs + [pl.BlockSpec((1, d), lambda i: (0, 0))],
        out_specs=pl.BlockSpec((tm, d), lambda i: (i, 0)),
        out_shape=jax.ShapeDtypeStruct((bounds[-1] * tm, d), BF16),
        compiler_params=_cparams(("parallel",)),
        name="prenorm",
    )(*srcs, g.reshape(1, d))


def _norm_proj_kernel(*refs, epilogue, n_extra, norm):
    x_ref, g_ref, w_ref = refs[:3]
    extra = refs[3:3 + n_extra]
    o_ref = refs[3 + n_extra]
    if norm:
        xf = x_ref[...]
        u = (xf * _rms_scale(xf) * g_ref[...]).astype(BF16)
    else:
        u = x_ref[...]
    acc = jnp.dot(u, w_ref[...], preferred_element_type=F32)
    o_ref[...] = epilogue(acc, *[e[...] for e in extra]).astype(o_ref.dtype)


def _norm_proj(x2d, g, w, epilogue, out_dtype, extras=(), extra_specs=(), tm=512, name="norm_proj", norm=True):
    m, d = x2d.shape
    n = w.shape[1]
    tm = min(tm, m)
    assert m % tm == 0
    in_specs = [
        pl.BlockSpec((tm, d), lambda i: (i, 0)),
        pl.BlockSpec((1, d), lambda i: (0, 0)),
        pl.BlockSpec((d, n), lambda i: (0, 0)),
    ] + list(extra_specs)
    return pl.pallas_call(
        functools.partial(_norm_proj_kernel, epilogue=epilogue, n_extra=len(extras), norm=norm),
        grid=(m // tm,),
        in_specs=in_specs,
        out_specs=pl.BlockSpec((tm, n), lambda i: (i, 0)),
        out_shape=jax.ShapeDtypeStruct((m, n), out_dtype),
        compiler_params=_cparams(("parallel",)),
        name=name,
    )(x2d, g.reshape(1, d), w, *extras)


def _hgrn_robust_chunk(r0, ls, q_ref, lf_ref, v_ref, tri_ref, st_ref, kc_ref, bc_ref, rev):
    C = CHUNK
    ng = C // SUB
    rowl = lax.broadcasted_iota(jnp.int32, (C, HEAD), 0)
    sub_pos = rowl & (SUB - 1)
    arow = lax.broadcasted_iota(jnp.int32, (C, C), 0)
    acol = lax.broadcasted_iota(jnp.int32, (C, C), 1)
    ones_b = jnp.ones((HEAD, HEAD), BF16)
    zeros_g = jnp.zeros((SUB, HEAD), F32)

    lf = lf_ref[0, pl.ds(r0, C), ls]
    q = q_ref[0, pl.ds(r0, C), ls].astype(F32)
    v = v_ref[0, pl.ds(r0, C), ls]
    k = 1.0 - jnp.exp(lf)
    hi = lf.astype(BF16)
    mid = (lf - hi.astype(F32)).astype(BF16)
    b = jnp.dot(tri_ref[...], jnp.concatenate([hi, mid], axis=0), preferred_element_type=F32)
    kc_ref[pl.ds(SUB, C), :] = k
    bc_ref[pl.ds(SUB, C), :] = b
    b_end = bc_ref[pl.ds(SUB + (0 if rev else C - 1), 1), :]

    st = st_ref[:, ls]
    qi = (q * jnp.exp(b)).astype(BF16)
    o = lax.dot_general(qi, st.astype(BF16), NT_DIMS, preferred_element_type=F32)
    kl = (k * jnp.exp(b_end - b)).astype(BF16)
    st_ref[:, ls] = st * jnp.exp(b_end) + lax.dot_general(v, kl, TN_DIMS, preferred_element_type=F32)

    a = jnp.zeros((C, C), F32)
    h = C // 2
    while h >= SUB:
        qparts, kparts = [], []
        for gi in range(ng):
            t0 = gi * SUB
            blk = t0 // (2 * h)
            in_upper = (t0 % (2 * h)) >= h
            is_query = (not in_upper) if rev else in_upper
            rr = blk * 2 * h + (h if rev else h - 1)
            bref = bc_ref[pl.ds(SUB + rr, 1), :]
            bg = b[t0:t0 + SUB]
            if is_query:
                qparts.append(q[t0:t0 + SUB] * jnp.exp(bg - bref))
                kparts.append(zeros_g)
            else:
                qparts.append(zeros_g)
                kparts.append(k[t0:t0 + SUB] * jnp.exp(bref - bg))
        qh = jnp.concatenate(qparts, axis=0).astype(BF16)
        kh = jnp.concatenate(kparts, axis=0).astype(BF16)
        ah = lax.dot_general(qh, kh, NT_DIMS, preferred_element_type=F32)
        if 2 * h < C:
            ah = jnp.where((arow ^ acol) < 2 * h, ah, 0.0)
        a = a + ah
        h //= 2

    for d in range(SUB):
        sh = SUB + (d if rev else -d)
        ks = kc_ref[pl.ds(sh, C), :]
        bs = bc_ref[pl.ds(sh, C), :]
        ok = (sub_pos + d <= SUB - 1) if rev else (sub_pos >= d)
        p = jnp.where(ok, q * ks * jnp.exp(b - bs), 0.0).astype(BF16)
        rs = jnp.dot(p, ones_b, preferred_element_type=F32)[:, :C]
        tgt = (arow + d) if rev else (arow - d)
        a = a + jnp.where(acol == tgt, rs, 0.0)

    return o + jnp.dot(a.astype(BF16), v, preferred_element_type=F32)


def _blockdiag2(x):
    z = jnp.zeros((x.shape[0], HEAD), x.dtype)
    return jnp.concatenate([jnp.concatenate([x[:, :HEAD], z], axis=1),
                            jnp.concatenate([z, x[:, HEAD:]], axis=1)], axis=0)


def _hgrn_kernel(*refs, rev, final, nchunk):
    if final:
        (q_ref, lf_ref, v_ref, tri_ref, sel_ref, of_ref, g_ref, gn_ref, o_ref,
         st_ref, qi_ref, a_ref, upd_ref, snap_ref, dec_ref, kc_ref, bc_ref) = refs
    else:
        (q_ref, lf_ref, v_ref, tri_ref, sel_ref, o_ref,
         st_ref, qi_ref, a_ref, upd_ref, snap_ref, dec_ref, kc_ref, bc_ref) = refs
        of_ref = g_ref = gn_ref = None
    C = CHUNK
    W = HG_PACK * HEAD

    @pl.when(pl.program_id(2) == 0)
    def _():
        st_ref[...] = jnp.zeros_like(st_ref)
        kc_ref[...] = jnp.zeros_like(kc_ref)
        bc_ref[...] = jnp.zeros_like(bc_ref)

    def emit(o, r0, ls):
        if final:
            tot = o + of_ref[0, pl.ds(r0, C), ls]
            parts = []
            for j in range(tot.shape[1] // HEAD):
                tj = tot[:, j * HEAD:(j + 1) * HEAD]
                parts.append(tj * _rms_scale(tj) * gn_ref[...])
            y = parts[0] if len(parts) == 1 else jnp.concatenate(parts, axis=1)
            y = y * g_ref[0, pl.ds(r0, C), ls].astype(F32)
            o_ref[0, pl.ds(r0, C), ls] = y.astype(o_ref.dtype)
        else:
            o_ref[0, pl.ds(r0, C), ls] = o

    sums = jnp.dot(sel_ref[...], lf_ref[0].astype(BF16), preferred_element_type=F32)
    safe = jnp.min(sums) >= SAFE_CHUNK_LOG_DECAY

    @pl.when(safe)
    def _():
        arow = lax.broadcasted_iota(jnp.int32, (C, 2 * C), 0)
        acol = lax.broadcasted_iota(jnp.int32, (C, 2 * C), 1) & (C - 1)
        causal = (acol >= arow) if rev else (acol <= arow)

        G = HGRN_UNROLL
        assert nchunk % G == 0

        def phase1(i, carry):
            cs = [i * G + g for g in range(G)]
            r0s = [pl.multiple_of(c * C, C) for c in cs]
            lfs = [lf_ref[0, pl.ds(r0, C), :] for r0 in r0s]
            his = [lf.astype(BF16) for lf in lfs]
            mids = [(lf - hi.astype(F32)).astype(BF16) for lf, hi in zip(lfs, his)]
            rhs = jnp.concatenate([jnp.concatenate(his, axis=1), jnp.concatenate(mids, axis=1)], axis=0)
            ball = jnp.dot(tri_ref[...], rhs, preferred_element_type=F32)
            qis, kls, kbars = [], [], []
            for g in range(G):
                b = ball[:, g * W:(g + 1) * W]
                k = 1.0 - jnp.exp(lfs[g])
                q = q_ref[0, pl.ds(r0s[g], C), :].astype(F32)
                qi = (q * jnp.exp(b)).astype(BF16)
                qi_ref[pl.ds(r0s[g], C), :] = qi
                b_end = b[0:1] if rev else b[C - 1:C]
                dec_ref[pl.ds(pl.multiple_of(cs[g] * SUB, SUB), SUB), :] = jnp.broadcast_to(jnp.exp(b_end), (SUB, W))
                qis.append(qi)
                kls.append((k * jnp.exp(b_end - b)).astype(BF16))
                kbars.append((k * jnp.exp(-b)).astype(BF16))
            avals = [lax.dot_general(qis[g], _blockdiag2(kbars[g]), NT_DIMS, preferred_element_type=F32)
                     for g in range(G)]
            for g in range(G):
                v = v_ref[0, pl.ds(r0s[g], C), :]
                for j in range(HG_PACK):
                    ls = slice(j * HEAD, (j + 1) * HEAD)
                    upd_ref[pl.ds(pl.multiple_of(cs[g] * HEAD, HEAD), HEAD), ls] = lax.dot_general(
                        v[:, ls], kls[g][:, ls], TN_DIMS, preferred_element_type=F32)
            for g in range(G):
                a_ref[pl.ds(r0s[g], C), :] = jnp.where(causal, avals[g], 0.0).astype(BF16)
            return carry

        lax.fori_loop(0, nchunk // G, phase1, 0)

        def phase2(i, carry):
            c = (nchunk - 1 - i) if rev else i
            s0 = pl.multiple_of(c * HEAD, HEAD)
            st = st_ref[...]
            snap_ref[pl.ds(s0, HEAD), :] = st.astype(BF16)
            dec = dec_ref[pl.ds(pl.multiple_of(c * SUB, SUB), 1), :]
            st_ref[...] = st * dec + upd_ref[pl.ds(s0, HEAD), :]
            return carry

        lax.fori_loop(0, nchunk, phase2, 0, unroll=G)

        def phase3(i, carry):
            cs = [i * G + g for g in range(G)]
            r0s = [pl.multiple_of(c * C, C) for c in cs]
            snaps = [snap_ref[pl.ds(pl.multiple_of(c * HEAD, HEAD), HEAD), :] for c in cs]
            o1 = [lax.dot_general(qi_ref[pl.ds(r0s[g], C), :], _blockdiag2(snaps[g]), NT_DIMS,
                                  preferred_element_type=F32) for g in range(G)]
            o2 = [jnp.dot(a_ref[pl.ds(r0s[g], C), :], _blockdiag2(v_ref[0, pl.ds(r0s[g], C), :]),
                          preferred_element_type=F32) for g in range(G)]
            for g in range(G):
                emit(o1[g] + o2[g], r0s[g], slice(None))
            return carry

        lax.fori_loop(0, nchunk // G, phase3, 0)

    @pl.when(jnp.logical_not(safe))
    def _():
        def body(i, carry):
            c = (nchunk - 1 - i) if rev else i
            r0 = pl.multiple_of(c * C, C)
            for j in range(HG_PACK):
                ls = slice(j * HEAD, (j + 1) * HEAD)
                o = _hgrn_robust_chunk(r0, ls, q_ref, lf_ref, v_ref, tri_ref, st_ref, kc_ref, bc_ref, rev)
                emit(o, r0, ls)
            return carry

        lax.fori_loop(0, nchunk, body, 0)


def _tri_matrix(rev):
    t = np.arange(CHUNK)
    m = (t[None, :] >= t[:, None]) if rev else (t[None, :] <= t[:, None])
    return jnp.asarray(np.concatenate([m, m], axis=1).astype(np.float32), dtype=BF16)


def _chunk_sum_matrix(rows):
    nchunk = rows // CHUNK
    nsel = -(-nchunk // 16) * 16
    m = np.zeros((nsel, rows), np.float32)
    m[np.arange(rows) // CHUNK, np.arange(rows)] = 1.0
    return jnp.asarray(m, dtype=BF16)


def _hgrn_pass(a_silu, a_lf, a_plain, gnorm, o_fwd, rev, rows):
    bsz, s, _ = a_silu.shape
    rows = min(rows, s)
    nblk = s // rows
    nchunk = rows // CHUNK
    w = HG_PACK * HEAD
    npk = HG_HEADS // HG_PACK
    final = o_fwd is not None
    seq = (lambda i: nblk - 1 - i) if rev else (lambda i: i)
    fcol = npk if rev else 0
    sel = _chunk_sum_matrix(rows)
    in_specs = [
        pl.BlockSpec((1, rows, w), lambda b, h, i: (b, seq(i), h)),
        pl.BlockSpec((1, rows, w), lambda b, h, i: (b, seq(i), h + fcol)),
        pl.BlockSpec((1, rows, w), lambda b, h, i: (b, seq(i), h)),
        pl.BlockSpec((CHUNK, 2 * CHUNK), lambda b, h, i: (0, 0)),
        pl.BlockSpec(sel.shape, lambda b, h, i: (0, 0)),
    ]
    args = [a_silu, a_lf, a_plain, _tri_matrix(rev), sel]
    if final:
        in_specs += [
            pl.BlockSpec((1, rows, w), lambda b, h, i: (b, seq(i), h)),
            pl.BlockSpec((1, rows, w), lambda b, h, i: (b, seq(i), h + npk)),
            pl.BlockSpec((1, HEAD), lambda b, h, i: (0, 0)),
        ]
        args += [o_fwd, a_silu, gnorm.reshape(1, HEAD)]
    return pl.pallas_call(
        functools.partial(_hgrn_kernel, rev=rev, final=final, nchunk=nchunk),
        grid=(bsz, npk, nblk),
        in_specs=in_specs,
        out_specs=pl.BlockSpec((1, rows, w), lambda b, h, i: (b, seq(i), h)),
        out_shape=jax.ShapeDtypeStruct((bsz, s, HG_HEADS * HEAD), BF16 if final else F32),
        scratch_shapes=[
            pltpu.VMEM((HEAD, w), F32),
            pltpu.VMEM((rows, w), BF16),
            pltpu.VMEM((rows, 2 * CHUNK), BF16),
            pltpu.VMEM((nchunk * HEAD, w), F32),
            pltpu.VMEM((nchunk * HEAD, w), BF16),
            pltpu.VMEM((nchunk * SUB, w), F32),
            pltpu.VMEM((CHUNK + 2 * SUB, HEAD), F32),
            pltpu.VMEM((CHUNK + 2 * SUB, HEAD), F32),
        ],
        compiler_params=_cparams(("parallel", "parallel", "arbitrary")),
        name="hgrn_bwd" if rev else "hgrn_fwd",
    )(*args)


def _attn_group(q_ref, k_ref, v_ref, qs_ref, ks_ref, vs_ref, acc_ref, m_ref, l_ref, *, dil, span, first):
    s = q_ref.shape[1]
    length = s // dil
    qt, kt = 2 * span, 4 * span
    ntile = length // qt
    dcol = (lax.broadcasted_iota(jnp.int32, (qt, kt), 1) - lax.broadcasted_iota(jnp.int32, (qt, kt), 0))
    if dil > 1:
        qs_ref[...] = q_ref[0].astype(F32)
        ks_ref[...] = k_ref[0].astype(F32)
        vs_ref[...] = v_ref[0].astype(F32)

    for r in range(dil):
        def tile(j, carry):
            w0 = jnp.clip(j * qt - span, 0, length - kt)
            qbase = pl.multiple_of(j * (qt * dil), qt * dil)
            kbase = pl.multiple_of(w0 * dil, span * dil)
            if dil > 1:
                rq = pl.ds(r, qt, stride=dil)
                rk = pl.ds(r, kt, stride=dil)
                q = qs_ref.at[pl.ds(qbase, qt * dil)][rq, :].astype(BF16)
                kw = ks_ref.at[pl.ds(kbase, kt * dil)][rk, :].astype(BF16)
                vw = vs_ref.at[pl.ds(kbase, kt * dil)][rk, :].astype(BF16)
            else:
                q = q_ref[0, pl.ds(qbase, qt), :]
                kw = k_ref[0, pl.ds(kbase, kt), :]
                vw = v_ref[0, pl.ds(kbase, kt), :]
            sc = lax.dot_general(q, kw, NT_DIMS, preferred_element_type=F32)
            sc = jnp.where(jnp.abs(dcol + (w0 - j * qt)) <= span, sc, NEG)
            mx = jnp.max(sc, axis=-1, keepdims=True)
            p = jnp.exp(sc - mx)
            ln = jnp.broadcast_to(jnp.sum(p, axis=-1, keepdims=True), (qt, HEAD))
            on = jnp.dot(p.astype(BF16), vw, preferred_element_type=F32)
            mn = jnp.broadcast_to(mx, (qt, HEAD))
            mv = m_ref.at[pl.ds(qbase, qt * dil)]
            lv = l_ref.at[pl.ds(qbase, qt * dil)]
            av = acc_ref.at[pl.ds(qbase, qt * dil)]
            rows = pl.ds(r, qt, stride=dil) if dil > 1 else slice(None)
            if first:
                mv[rows, :] = mn
                lv[rows, :] = ln
                av[rows, :] = on
            else:
                mo = mv[rows, :]
                mm = jnp.maximum(mo, mn)
                wo = jnp.exp(mo - mm)
                wn = jnp.exp(mn - mm)
                mv[rows, :] = mm
                lv[rows, :] = wo * lv[rows, :] + wn * ln
                av[rows, :] = wo * av[rows, :] + wn * on
            return carry

        lax.fori_loop(0, ntile, tile, 0, unroll=min(ATTN_UNROLL, ntile))


def _attn_kernel(q_ref, k_ref, v_ref, o_ref, qs_ref, ks_ref, vs_ref, acc_ref, m_ref, l_ref):
    g = pl.program_id(2)
    for gi, (win, dil) in enumerate(ATTN_GROUPS):
        @pl.when(g == gi)
        def _(gi=gi, win=win, dil=dil):
            _attn_group(q_ref, k_ref, v_ref, qs_ref, ks_ref, vs_ref, acc_ref, m_ref, l_ref,
                        dil=dil, span=(win // 2) // dil, first=(gi == 0))

    @pl.when(g == len(ATTN_GROUPS) - 1)
    def _():
        o_ref[0] = (acc_ref[...] * (1.0 / l_ref[...])).astype(o_ref.dtype)


def _dilated_attention(a_rot, a_plain):
    bsz, s, _ = a_rot.shape
    ng = len(ATTN_GROUPS)
    vbase = HG_HEADS
    blk = (1, s, HEAD)
    return pl.pallas_call(
        _attn_kernel,
        grid=(bsz, ATTN_HEADS, ng),
        in_specs=[pl.BlockSpec(blk, lambda b, h, g: (b, 0, 2 * g * ATTN_HEADS + h)),
                  pl.BlockSpec(blk, lambda b, h, g: (b, 0, (2 * g + 1) * ATTN_HEADS + h)),
                  pl.BlockSpec(blk, lambda b, h, g: (b, 0, vbase + g * ATTN_HEADS + h))],
        out_specs=pl.BlockSpec(blk, lambda b, h, g: (b, 0, h)),
        out_shape=jax.ShapeDtypeStruct((bsz, s, ATTN_WIDTH), BF16),
        scratch_shapes=[pltpu.VMEM((s, HEAD), F32)] * 6,
        compiler_params=_cparams(("parallel", "parallel", "arbitrary")),
        name="dilated_attn",
    )(a_rot, a_rot, a_plain)


def _merge_kernel(*refs, bounds):
    nsrc = len(bounds) - 1
    hg_ref, attn_ref, gate_ref, who_ref, wao_ref, wout_ref, out_ref = refs[nsrc:]
    x = _pick_source(pl.program_id(0), refs[:nsrc], bounds)
    d = x.shape[1]
    yh = jnp.dot(hg_ref[...], who_ref[...], preferred_element_type=F32)
    ya = jnp.dot(attn_ref[...], wao_ref[...], preferred_element_type=F32)
    merged = gate_ref[:, :d].astype(F32) * yh + gate_ref[:, d:].astype(F32) * ya
    out_ref[...] = x + jnp.dot(merged.astype(BF16), wout_ref[...], preferred_element_type=F32)


def _merge(srcs, hg, attn, gates, who, wao, wout, tm=512):
    d = srcs[0].shape[1]
    xspecs, bounds = _row_sources(srcs, tm)
    row = lambda w: pl.BlockSpec((tm, w), lambda i: (i, 0))
    full = lambda a: pl.BlockSpec(a.shape, lambda i: (0, 0))
    return pl.pallas_call(
        functools.partial(_merge_kernel, bounds=bounds),
        grid=(bounds[-1],),
        in_specs=xspecs + [row(d), row(ATTN_WIDTH), row(2 * d), full(who), full(wao), full(wout)],
        out_specs=row(d),
        out_shape=jax.ShapeDtypeStruct((bounds[-1] * tm, d), F32),
        compiler_params=_cparams(("parallel",)),
        name="merge",
    )(*srcs, hg, attn, gates, who, wao, wout)


def _xattn_kernel(h_ref, g_ref, wq_ref, kv_ref, wo_ref, out_ref):
    hx = h_ref[0]
    d = hx.shape[1]
    hd = d // XA_HEADS
    u = (hx * _rms_scale(hx) * g_ref[...]).astype(BF16)
    q = (jnp.dot(u, wq_ref[...], preferred_element_type=F32) * (hd ** -0.5)).astype(BF16)
    outs = []
    for h in range(XA_HEADS):
        kh = kv_ref[0, :, h * hd:(h + 1) * hd]
        vh = kv_ref[0, :, d + h * hd:d + (h + 1) * hd]
        s = lax.dot_general(q[:, h * hd:(h + 1) * hd], kh, NT_DIMS, preferred_element_type=F32)
        p = jnp.exp(s - jnp.max(s, axis=-1, keepdims=True))
        den = jnp.sum(p, axis=-1, keepdims=True)
        outs.append(jnp.dot(p.astype(BF16), vh, preferred_element_type=F32) * (1.0 / den))
    o = jnp.concatenate(outs, axis=1).astype(BF16)
    out_ref[0] = hx + jnp.dot(o, wo_ref[...], preferred_element_type=F32)


def _xattn(h3d, g, wq, kv, wo, tm=512):
    bsz, s, d = h3d.shape
    tm = min(tm, s)
    full = lambda a: pl.BlockSpec(a.shape, lambda b, i: (0, 0))
    return pl.pallas_call(
        _xattn_kernel,
        grid=(bsz, s // tm),
        in_specs=[pl.BlockSpec((1, tm, d), lambda b, i: (b, i, 0)),
                  pl.BlockSpec((1, d), lambda b, i: (0, 0)),
                  full(wq),
                  pl.BlockSpec((1,) + kv.shape[1:], lambda b, i: (b, 0, 0)),
                  full(wo)],
        out_specs=pl.BlockSpec((1, tm, d), lambda b, i: (b, i, 0)),
        out_shape=jax.ShapeDtypeStruct((bsz, s, d), F32),
        compiler_params=_cparams(("parallel", "parallel")),
        name="xattn",
    )(h3d, g.reshape(1, d), wq, kv, wo)


def _mlp_kernel(h_ref, g_ref, w1_ref, w2_ref, gf_ref, *out_refs, bounds):
    hx = h_ref[...]
    u = (hx * _rms_scale(hx) * g_ref[...]).astype(BF16)
    a = jnp.maximum(jnp.dot(u, w1_ref[...], preferred_element_type=F32), 0.0)
    a = (a * a).astype(BF16)
    y = hx + jnp.dot(a, w2_ref[...], preferred_element_type=F32)
    y = y * _rms_scale(y) * gf_ref[...]
    i = pl.program_id(0)
    for k, out_ref in enumerate(out_refs):
        @pl.when(jnp.logical_and(i >= bounds[k], i < bounds[k + 1]))
        def _(out_ref=out_ref):
            out_ref[...] = y


def _mlp(h2d, g, w1, w2, gf, group_rows, tm=512):
    m, d = h2d.shape
    assert sum(group_rows) == m
    bounds = [0]
    for r in group_rows:
        assert r % tm == 0
        bounds.append(bounds[-1] + r // tm)
    full = lambda a: pl.BlockSpec(a.shape, lambda i: (0, 0))
    out_specs = [pl.BlockSpec((tm, d), lambda i, lo=bounds[k], n=bounds[k + 1] - bounds[k]:
                              (jnp.clip(i - lo, 0, n - 1), 0)) for k in range(len(group_rows))]
    return pl.pallas_call(
        functools.partial(_mlp_kernel, bounds=bounds),
        grid=(m // tm,),
        in_specs=[pl.BlockSpec((tm, d), lambda i: (i, 0)), pl.BlockSpec((1, d), lambda i: (0, 0)),
                  full(w1), full(w2), pl.BlockSpec((1, d), lambda i: (0, 0))],
        out_specs=out_specs,
        out_shape=[jax.ShapeDtypeStruct((r, d), F32) for r in group_rows],
        compiler_params=_cparams(("arbitrary",)),
        name="mlp_final",
    )(h2d, g.reshape(1, d), w1, w2, gf.reshape(1, d))


def _rotary_tables(s):
    half = ROT_DIM // 2
    inv = ROPE_THETA ** (-jnp.arange(half, dtype=F32) * 2.0 / ROT_DIM)
    ang = jnp.arange(s, dtype=F32)[:, None] * inv[None, :]
    cos, sin = jnp.cos(ang), jnp.sin(ang)
    pad = jnp.zeros((s, HEAD - ROT_DIM), F32)
    zero = jnp.zeros((s, half), F32)
    cos_t = jnp.concatenate([cos, cos, pad + 1.0], axis=1)
    sin_lo = jnp.concatenate([-sin, zero, pad], axis=1)
    sin_hi = jnp.concatenate([zero, sin, pad], axis=1)
    return cos_t, sin_lo, sin_hi


def _encode(xs, mems, mix_norm_g, w_in, hgrn_lb_logits, hgrn_gnorm_g, w_hgrn_o, w_attn_o, w_out,
            xa_norm_g, mem_norm_g, w_xq, w_xkv, w_xo, ffn_norm_g, w_ffn1, w_ffn2, final_norm_g,
            hgrn_rows=2048):
    s, d = xs[0].shape[1:]
    bsz = sum(x.shape[0] for x in xs)
    t = bsz * s
    depth = w_in.shape[0]
    assert depth == 1, "the final norm is fused into the (single) layer's MLP call"
    l = 0
    fd = HG_HEADS * HEAD
    lb_all = jnp.cumsum(jax.nn.softmax(hgrn_lb_logits.astype(F32), axis=1), axis=1)
    sizes = (fd,) * 5 + (ATTN_WIDTH,) * 9 + (d, d)
    offs = np.concatenate([[0], np.cumsum(sizes)])
    cos_t, sin_lo, sin_hi = _rotary_tables(s)
    tm = min(TOKEN_TILE, s)
    tmp = min(PROJ_TOKEN_TILE, s)
    tabspec = pl.BlockSpec((tmp, HEAD), lambda i: (i % (s // tmp), 0))
    srcs = [x.reshape(-1, d) for x in xs]
    mem = jnp.concatenate(mems, axis=0) if len(mems) > 1 else mems[0]

    wl = w_in[l]
    seg = lambda p: wl[:, offs[p]:offs[p + 1]]
    bf = lambda a: a.astype(BF16)
    w_silu = bf(jnp.concatenate([seg(0), seg(4)], axis=1))
    w_lf = bf(jnp.concatenate([seg(1), seg(2)], axis=1))
    w_plain = bf(jnp.concatenate([seg(3), seg(7), seg(10), seg(13)], axis=1))
    w_rot = bf(jnp.concatenate([seg(5), seg(6), seg(8), seg(9), seg(11), seg(12)], axis=1))
    w_gate = bf(jnp.concatenate([seg(14), seg(15)], axis=1))
    g_mix = mix_norm_g[l]

    u = _prenorm(srcs, g_mix, tm)
    proj = functools.partial(_norm_proj, u, g_mix, tm=tmp, norm=False)
    a_silu = proj(w_silu, _ep_silu, BF16, name="proj_silu")
    lb_row = lb_all[:, l].reshape(1, 2 * fd)
    a_lf = proj(w_lf, _ep_logf, F32, extras=(lb_row,),
                extra_specs=(pl.BlockSpec((1, 2 * fd), lambda i: (0, 0)),), name="proj_logf")
    a_plain = proj(w_plain, _ep_plain, BF16, name="proj_plain")
    qscale = jnp.tile(jnp.concatenate([jnp.full((ATTN_WIDTH,), HEAD ** -0.5, F32),
                                       jnp.ones((ATTN_WIDTH,), F32)]), 3).reshape(1, 6 * ATTN_WIDTH)
    a_rot = proj(w_rot, _ep_rotary, BF16, extras=(cos_t, sin_lo, sin_hi, qscale),
                 extra_specs=(tabspec, tabspec, tabspec, pl.BlockSpec((1, 6 * ATTN_WIDTH), lambda i: (0, 0))),
                 name="proj_rotary")
    a_gate = proj(w_gate, _ep_sigmoid, BF16, name="proj_gate")

    a_silu3 = a_silu.reshape(bsz, s, 2 * fd)
    a_lf3 = a_lf.reshape(bsz, s, 2 * fd)
    a_plain3 = a_plain.reshape(bsz, s, -1)
    o_fwd = _hgrn_pass(a_silu3, a_lf3, a_plain3, hgrn_gnorm_g[l], None, False, hgrn_rows)
    hg = _hgrn_pass(a_silu3, a_lf3, a_plain3, hgrn_gnorm_g[l], o_fwd, True, hgrn_rows)

    attn = _dilated_attention(a_rot.reshape(bsz, s, -1), a_plain3)

    h2d = _merge(srcs, hg.reshape(t, fd), attn.reshape(t, ATTN_WIDTH), a_gate,
                 w_hgrn_o[l].astype(BF16), w_attn_o[l].astype(BF16), w_out[l].astype(BF16), tm=tm)

    nm = mem.shape[1]
    kv = _norm_proj(mem.reshape(bsz * nm, d), mem_norm_g[l], w_xkv[l].astype(BF16), _ep_plain, BF16,
                    tm=nm, name="proj_memkv")
    h3d = _xattn(h2d.reshape(bsz, s, d), xa_norm_g[l], w_xq[l].astype(BF16), kv.reshape(bsz, nm, 2 * d),
                 w_xo[l].astype(BF16), tm=tm)
    outs = _mlp(h3d.reshape(t, d), ffn_norm_g[l], w_ffn1[l].astype(BF16), w_ffn2[l].astype(BF16), final_norm_g,
                [x.shape[0] * s for x in xs], tm=tm)
    return [o.reshape(x.shape) for o, x in zip(outs, xs)]


def kernel(x_prompt, x_sample, mem_prompt, mem_sample, mix_norm_g, w_in, hgrn_lb_logits, hgrn_gnorm_g,
           w_hgrn_o, w_attn_o, w_out, xa_norm_g, mem_norm_g, w_xq, w_xkv, w_xo, ffn_norm_g, w_ffn1,
           w_ffn2, final_norm_g):
    assert x_prompt.shape[1:] == x_sample.shape[1:]
    y_prompt, y_sample = _encode(
        [x_prompt, x_sample], [mem_prompt, mem_sample], mix_norm_g, w_in, hgrn_lb_logits, hgrn_gnorm_g,
        w_hgrn_o, w_attn_o, w_out, xa_norm_g, mem_norm_g, w_xq, w_xkv, w_xo, ffn_norm_g, w_ffn1, w_ffn2,
        final_norm_g)
    return y_prompt, y_sample
```

```python
import functools

import numpy as np
import jax
import jax.numpy as jnp
from jax import lax
from jax.experimental import pallas as pl
from jax.experimental.pallas import tpu as pltpu

F32 = jnp.float32
BF16 = jnp.bfloat16

RMS_EPS = 1e-6
ROPE_THETA = 500000.0
HG_HEADS = 8
HEAD = 128
ATTN_HEADS = 4
ATTN_WIDTH = ATTN_HEADS * HEAD
ATTN_GROUPS = ((128, 1), (512, 4), (2048, 16))
ROT_DIM = HEAD // 4
XA_HEADS = 4
CHUNK = 64
SUB = 8
NEG = -1e30
HG_PACK = 2
SAFE_CHUNK_LOG_DECAY = -60.0
HGRN_UNROLL = 4
ATTN_UNROLL = 4
NT_DIMS = (((1,), (1,)), ((), ()))
TN_DIMS = (((0,), (0,)), ((), ()))

TOKEN_TILE = 512
PROJ_TOKEN_TILE = 1024
PROJ_COL_CHUNK = 512
ROW_SUBTILE = 256

VMEM_LIMIT = 56 * 1024 * 1024


def _cparams(sem):
    return pltpu.CompilerParams(dimension_semantics=sem, vmem_limit_bytes=VMEM_LIMIT)


def _sigmoid(x):
    return 1.0 / (1.0 + jnp.exp(-x))


def _rms_scale(xf):
    return lax.rsqrt(jnp.mean(xf * xf, axis=-1, keepdims=True) + RMS_EPS)


def _ep_plain(acc):
    return acc


def _ep_silu(acc):
    return acc * _sigmoid(acc)


def _ep_sigmoid(acc):
    return _sigmoid(acc)


def _ep_logf(acc, lb):
    return jnp.log(lb + (1.0 - lb) * _sigmoid(acc))


def _ep_rotary(acc, cos, sin_lo, sin_hi, colscale):
    half = ROT_DIM // 2
    outs = []
    for c in range(acc.shape[1] // HEAD):
        t = acc[:, c * HEAD:(c + 1) * HEAD]
        r = t * cos + pltpu.roll(t, HEAD - half, 1) * sin_lo + pltpu.roll(t, half, 1) * sin_hi
        outs.append(r)
    return jnp.concatenate(outs, axis=1) * colscale


def _row_sources(srcs, tm):
    bounds = [0]
    for a in srcs:
        assert a.shape[0] % tm == 0
        bounds.append(bounds[-1] + a.shape[0] // tm)
    specs = [pl.BlockSpec((tm, a.shape[1]),
                          lambda i, lo=bounds[k], n=bounds[k + 1] - bounds[k]: (jnp.clip(i - lo, 0, n - 1), 0))
             for k, a in enumerate(srcs)]
    return specs, bounds


def _pick_source(i, refs, bounds, rows=slice(None)):
    x = refs[-1][rows, :]
    for k in range(len(refs) - 2, -1, -1):
        x = jnp.where(i < bounds[k + 1], refs[k][rows, :], x)
    return x


def _prenorm_kernel(*refs, bounds):
    nsrc = len(bounds) - 1
    g_ref, o_ref = refs[nsrc], refs[nsrc + 1]
    xf = _pick_source(pl.program_id(0), refs[:nsrc], bounds)
    o_ref[...] = (xf * _rms_scale(xf) * g_ref[...]).astype(o_ref.dtype)


def _prenorm(srcs, g, tm):
    d = srcs[0].shape[1]
    specs, bounds = _row_sources(srcs, tm)
    return pl.pallas_call(
        functools.partial(_prenorm_kernel, bounds=bounds),
        grid=(bounds[-1],),
        in_specs=specs + [pl.BlockSpec((1, d), lambda i: (0, 0))],
        out_specs=pl.BlockSpec((tm, d), lambda i: (i, 0)),
        out_shape=jax.ShapeDtypeStruct((bounds[-1] * tm, d), BF16),
        compiler_params=_cparams(("parallel",)),
        name="prenorm",
    )(*srcs, g.reshape(1, d))


def _norm_proj_kernel(*refs, epilogue, n_extra, norm):
    x_ref, g_ref, w_ref = refs[:3]
    extra = refs[3:3 + n_extra]
    o_ref = refs[3 + n_extra]
    if norm:
        xf = x_ref[...]
        u = (xf * _rms_scale(xf) * g_ref[...]).astype(BF16)
    else:
        u = x_ref[...]
    n = w_ref.shape[1]
    ncol = PROJ_COL_CHUNK if n % PROJ_COL_CHUNK == 0 else n
    for c0 in range(0, n, ncol):
        cs = slice(c0, c0 + ncol)
        acc = jnp.dot(u, w_ref[:, cs], preferred_element_type=F32)
        ex = [e[:, cs] if e.shape[-1] == n else e[...] for e in extra]
        o_ref[:, cs] = epilogue(acc, *ex).astype(o_ref.dtype)


def _norm_proj(x2d, g, w, epilogue, out_dtype, extras=(), extra_specs=(), tm=512, name="norm_proj", norm=True):
    m, d = x2d.shape
    n = w.shape[1]
    tm = min(tm, m)
    assert m % tm == 0
    in_specs = [
        pl.BlockSpec((tm, d), lambda i: (i, 0)),
        pl.BlockSpec((1, d), lambda i: (0, 0)),
        pl.BlockSpec((d, n), lambda i: (0, 0)),
    ] + list(extra_specs)
    return pl.pallas_call(
        functools.partial(_norm_proj_kernel, epilogue=epilogue, n_extra=len(extras), norm=norm),
        grid=(m // tm,),
        in_specs=in_specs,
        out_specs=pl.BlockSpec((tm, n), lambda i: (i, 0)),
        out_shape=jax.ShapeDtypeStruct((m, n), out_dtype),
        compiler_params=_cparams(("parallel",)),
        name=name,
    )(x2d, g.reshape(1, d), w, *extras)


def _hgrn_robust_chunk(r0, ls, q_ref, lf_ref, v_ref, tri_ref, st_ref, kc_ref, bc_ref, rev):
    C = CHUNK
    ng = C // SUB
    rowl = lax.broadcasted_iota(jnp.int32, (C, HEAD), 0)
    sub_pos = rowl & (SUB - 1)
    arow = lax.broadcasted_iota(jnp.int32, (C, C), 0)
    acol = lax.broadcasted_iota(jnp.int32, (C, C), 1)
    ones_b = jnp.ones((HEAD, HEAD), BF16)
    zeros_g = jnp.zeros((SUB, HEAD), F32)

    lf = lf_ref[0, pl.ds(r0, C), ls]
    q = q_ref[0, pl.ds(r0, C), ls].astype(F32)
    v = v_ref[0, pl.ds(r0, C), ls]
    k = 1.0 - jnp.exp(lf)
    hi = lf.astype(BF16)
    mid = (lf - hi.astype(F32)).astype(BF16)
    b = jnp.dot(tri_ref[...], jnp.concatenate([hi, mid], axis=0), preferred_element_type=F32)
    kc_ref[pl.ds(SUB, C), :] = k
    bc_ref[pl.ds(SUB, C), :] = b
    b_end = bc_ref[pl.ds(SUB + (0 if rev else C - 1), 1), :]

    st = st_ref[:, ls]
    qi = (q * jnp.exp(b)).astype(BF16)
    o = lax.dot_general(qi, st.astype(BF16), NT_DIMS, preferred_element_type=F32)
    kl = (k * jnp.exp(b_end - b)).astype(BF16)
    st_ref[:, ls] = st * jnp.exp(b_end) + lax.dot_general(v, kl, TN_DIMS, preferred_element_type=F32)

    a = jnp.zeros((C, C), F32)
    h = C // 2
    while h >= SUB:
        qparts, kparts = [], []
        for gi in range(ng):
            t0 = gi * SUB
            blk = t0 // (2 * h)
            in_upper = (t0 % (2 * h)) >= h
            is_query = (not in_upper) if rev else in_upper
            rr = blk * 2 * h + (h if rev else h - 1)
            bref = bc_ref[pl.ds(SUB + rr, 1), :]
            bg = b[t0:t0 + SUB]
            if is_query:
                qparts.append(q[t0:t0 + SUB] * jnp.exp(bg - bref))
                kparts.append(zeros_g)
            else:
                qparts.append(zeros_g)
                kparts.append(k[t0:t0 + SUB] * jnp.exp(bref - bg))
        qh = jnp.concatenate(qparts, axis=0).astype(BF16)
        kh = jnp.concatenate(kparts, axis=0).astype(BF16)
        ah = lax.dot_general(qh, kh, NT_DIMS, preferred_element_type=F32)
        if 2 * h < C:
            ah = jnp.where((arow ^ acol) < 2 * h, ah, 0.0)
        a = a + ah
        h //= 2

    for d in range(SUB):
        sh = SUB + (d if rev else -d)
        ks = kc_ref[pl.ds(sh, C), :]
        bs = bc_ref[pl.ds(sh, C), :]
        ok = (sub_pos + d <= SUB - 1) if rev else (sub_pos >= d)
        p = jnp.where(ok, q * ks * jnp.exp(b - bs), 0.0).astype(BF16)
        rs = jnp.dot(p, ones_b, preferred_element_type=F32)[:, :C]
        tgt = (arow + d) if rev else (arow - d)
        a = a + jnp.where(acol == tgt, rs, 0.0)

    return o + jnp.dot(a.astype(BF16), v, preferred_element_type=F32)


def _blockdiag2(x):
    z = jnp.zeros((x.shape[0], HEAD), x.dtype)
    return jnp.concatenate([jnp.concatenate([x[:, :HEAD], z], axis=1),
                            jnp.concatenate([z, x[:, HEAD:]], axis=1)], axis=0)


def _hgrn_kernel(*refs, rev, final, nchunk):
    if final:
        (q_ref, lf_ref, v_ref, tri_ref, sel_ref, of_ref, g_ref, gn_ref, o_ref,
         st_ref, qi_ref, a_ref, upd_ref, snap_ref, dec_ref, b_ref, kc_ref, bc_ref) = refs
    else:
        (q_ref, lf_ref, v_ref, tri_ref, sel_ref, o_ref,
         st_ref, qi_ref, a_ref, upd_ref, snap_ref, dec_ref, b_ref, kc_ref, bc_ref) = refs
        of_ref = g_ref = gn_ref = None
    C = CHUNK
    W = HG_PACK * HEAD

    @pl.when(pl.program_id(2) == 0)
    def _():
        st_ref[...] = jnp.zeros_like(st_ref)
        kc_ref[...] = jnp.zeros_like(kc_ref)
        bc_ref[...] = jnp.zeros_like(bc_ref)

    def emit(o, r0, ls):
        if final:
            tot = o + of_ref[0, pl.ds(r0, C), ls]
            parts = []
            for j in range(tot.shape[1] // HEAD):
                tj = tot[:, j * HEAD:(j + 1) * HEAD]
                parts.append(tj * _rms_scale(tj) * gn_ref[...])
            y = parts[0] if len(parts) == 1 else jnp.concatenate(parts, axis=1)
            y = y * g_ref[0, pl.ds(r0, C), ls].astype(F32)
            o_ref[0, pl.ds(r0, C), ls] = y.astype(o_ref.dtype)
        else:
            o_ref[0, pl.ds(r0, C), ls] = o

    sums = jnp.dot(sel_ref[...], lf_ref[0].astype(BF16), preferred_element_type=F32)
    safe = jnp.min(sums) >= SAFE_CHUNK_LOG_DECAY

    @pl.when(safe)
    def _():
        arow = lax.broadcasted_iota(jnp.int32, (C, 2 * C), 0)
        acol = lax.broadcasted_iota(jnp.int32, (C, 2 * C), 1) & (C - 1)
        causal = (acol >= arow) if rev else (acol <= arow)

        G = HGRN_UNROLL
        assert nchunk % G == 0

        G2 = 2 * G
        assert nchunk % G2 == 0

        def phase0(i, carry):
            r0s = [pl.multiple_of((i * G2 + g) * C, C) for g in range(G2)]
            lfs = [lf_ref[0, pl.ds(r0, C), :] for r0 in r0s]
            his = [lf.astype(BF16) for lf in lfs]
            mids = [(lf - hi.astype(F32)).astype(BF16) for lf, hi in zip(lfs, his)]
            rhs = jnp.concatenate([jnp.concatenate(his, axis=1), jnp.concatenate(mids, axis=1)], axis=0)
            ball = jnp.dot(tri_ref[...], rhs, preferred_element_type=F32)
            for g in range(G2):
                b_ref[pl.ds(r0s[g], C), :] = ball[:, g * W:(g + 1) * W]
            return carry

        lax.fori_loop(0, nchunk // G2, phase0, 0)

        def phase1(i, carry):
            cs = [i * G + g for g in range(G)]
            r0s = [pl.multiple_of(c * C, C) for c in cs]
            lfs = [lf_ref[0, pl.ds(r0, C), :] for r0 in r0s]
            qis, kls, kbars = [], [], []
            for g in range(G):
                b = b_ref[pl.ds(r0s[g], C), :]
                k = 1.0 - jnp.exp(lfs[g])
                q = q_ref[0, pl.ds(r0s[g], C), :].astype(F32)
                qi = (q * jnp.exp(b)).astype(BF16)
                qi_ref[pl.ds(r0s[g], C), :] = qi
                b_end = b[0:1] if rev else b[C - 1:C]
                dec_ref[pl.ds(pl.multiple_of(cs[g] * SUB, SUB), SUB), :] = jnp.broadcast_to(jnp.exp(b_end), (SUB, W))
                qis.append(qi)
                kls.append((k * jnp.exp(b_end - b)).astype(BF16))
                kbars.append((k * jnp.exp(-b)).astype(BF16))
            avals = [lax.dot_general(qis[g], _blockdiag2(kbars[g]), NT_DIMS, preferred_element_type=F32)
                     for g in range(G)]
            for g in range(G):
                v = v_ref[0, pl.ds(r0s[g], C), :]
                for j in range(HG_PACK):
                    ls = slice(j * HEAD, (j + 1) * HEAD)
                    upd_ref[pl.ds(pl.multiple_of(cs[g] * HEAD, HEAD), HEAD), ls] = lax.dot_general(
                        v[:, ls], kls[g][:, ls], TN_DIMS, preferred_element_type=F32)
            for g in range(G):
                a_ref[pl.ds(r0s[g], C), :] = jnp.where(causal, avals[g], 0.0).astype(BF16)
            return carry

        lax.fori_loop(0, nchunk // G, phase1, 0)

        def phase2(i, carry):
            c = (nchunk - 1 - i) if rev else i
            s0 = pl.multiple_of(c * HEAD, HEAD)
            st = st_ref[...]
            snap_ref[pl.ds(s0, HEAD), :] = st.astype(BF16)
            dec = dec_ref[pl.ds(pl.multiple_of(c * SUB, SUB), 1), :]
            st_ref[...] = st * dec + upd_ref[pl.ds(s0, HEAD), :]
            return carry

        lax.fori_loop(0, nchunk, phase2, 0, unroll=G)

        def phase3(i, carry):
            cs = [i * G + g for g in range(G)]
            r0s = [pl.multiple_of(c * C, C) for c in cs]
            snaps = [snap_ref[pl.ds(pl.multiple_of(c * HEAD, HEAD), HEAD), :] for c in cs]
            o1 = [lax.dot_general(qi_ref[pl.ds(r0s[g], C), :], _blockdiag2(snaps[g]), NT_DIMS,
                                  preferred_element_type=F32) for g in range(G)]
            o2 = [jnp.dot(a_ref[pl.ds(r0s[g], C), :], _blockdiag2(v_ref[0, pl.ds(r0s[g], C), :]),
                          preferred_element_type=F32) for g in range(G)]
            for g in range(G):
                emit(o1[g] + o2[g], r0s[g], slice(None))
            return carry

        lax.fori_loop(0, nchunk // G, phase3, 0)

    @pl.when(jnp.logical_not(safe))
    def _():
        def body(i, carry):
            c = (nchunk - 1 - i) if rev else i
            r0 = pl.multiple_of(c * C, C)
            for j in range(HG_PACK):
                ls = slice(j * HEAD, (j + 1) * HEAD)
                o = _hgrn_robust_chunk(r0, ls, q_ref, lf_ref, v_ref, tri_ref, st_ref, kc_ref, bc_ref, rev)
                emit(o, r0, ls)
            return carry

        lax.fori_loop(0, nchunk, body, 0)


def _tri_matrix(rev):
    t = np.arange(CHUNK)
    m = (t[None, :] >= t[:, None]) if rev else (t[None, :] <= t[:, None])
    return jnp.asarray(np.concatenate([m, m], axis=1).astype(np.float32), dtype=BF16)


def _chunk_sum_matrix(rows):
    nchunk = rows // CHUNK
    nsel = -(-nchunk // 16) * 16
    m = np.zeros((nsel, rows), np.float32)
    m[np.arange(rows) // CHUNK, np.arange(rows)] = 1.0
    return jnp.asarray(m, dtype=BF16)


def _hgrn_pass(a_silu, a_lf, a_plain, gnorm, o_fwd, rev, rows):
    bsz, s, _ = a_silu.shape
    rows = min(rows, s)
    nblk = s // rows
    nchunk = rows // CHUNK
    w = HG_PACK * HEAD
    npk = HG_HEADS // HG_PACK
    final = o_fwd is not None
    seq = (lambda i: nblk - 1 - i) if rev else (lambda i: i)
    fcol = npk if rev else 0
    sel = _chunk_sum_matrix(rows)
    in_specs = [
        pl.BlockSpec((1, rows, w), lambda b, h, i: (b, seq(i), h)),
        pl.BlockSpec((1, rows, w), lambda b, h, i: (b, seq(i), h + fcol)),
        pl.BlockSpec((1, rows, w), lambda b, h, i: (b, seq(i), h)),
        pl.BlockSpec((CHUNK, 2 * CHUNK), lambda b, h, i: (0, 0)),
        pl.BlockSpec(sel.shape, lambda b, h, i: (0, 0)),
    ]
    args = [a_silu, a_lf, a_plain, _tri_matrix(rev), sel]
    if final:
        in_specs += [
            pl.BlockSpec((1, rows, w), lambda b, h, i: (b, seq(i), h)),
            pl.BlockSpec((1, rows, w), lambda b, h, i: (b, seq(i), h + npk)),
            pl.BlockSpec((1, HEAD), lambda b, h, i: (0, 0)),
        ]
        args += [o_fwd, a_silu, gnorm.reshape(1, HEAD)]
    return pl.pallas_call(
        functools.partial(_hgrn_kernel, rev=rev, final=final, nchunk=nchunk),
        grid=(bsz, npk, nblk),
        in_specs=in_specs,
        out_specs=pl.BlockSpec((1, rows, w), lambda b, h, i: (b, seq(i), h)),
        out_shape=jax.ShapeDtypeStruct((bsz, s, HG_HEADS * HEAD), BF16 if final else F32),
        scratch_shapes=[
            pltpu.VMEM((HEAD, w), F32),
            pltpu.VMEM((rows, w), BF16),
            pltpu.VMEM((rows, 2 * CHUNK), BF16),
            pltpu.VMEM((nchunk * HEAD, w), F32),
            pltpu.VMEM((nchunk * HEAD, w), BF16),
            pltpu.VMEM((nchunk * SUB, w), F32),
            pltpu.VMEM((rows, w), F32),
            pltpu.VMEM((CHUNK + 2 * SUB, HEAD), F32),
            pltpu.VMEM((CHUNK + 2 * SUB, HEAD), F32),
        ],
        compiler_params=_cparams(("parallel", "parallel", "arbitrary")),
        name="hgrn_bwd" if rev else "hgrn_fwd",
    )(*args)


def _attn_group(q_ref, k_ref, v_ref, qs_ref, ks_ref, vs_ref, acc_ref, m_ref, l_ref, *, dil, span, first):
    s = q_ref.shape[1]
    length = s // dil
    qt, kt = 2 * span, 4 * span
    ntile = length // qt
    dcol = (lax.broadcasted_iota(jnp.int32, (qt, kt), 1) - lax.broadcasted_iota(jnp.int32, (qt, kt), 0))
    if dil > 1:
        qs_ref[...] = q_ref[0].astype(F32)
        ks_ref[...] = k_ref[0].astype(F32)
        vs_ref[...] = v_ref[0].astype(F32)

    G = min(ATTN_UNROLL, ntile)
    assert ntile % G == 0
    for r in range(dil):
        def tiles(i, carry):
            js = [i * G + g for g in range(G)]
            w0s = [jnp.clip(j * qt - span, 0, length - kt) for j in js]
            qbases = [pl.multiple_of(j * (qt * dil), qt * dil) for j in js]
            kbases = [pl.multiple_of(w0 * dil, span * dil) for w0 in w0s]
            rows = pl.ds(r, qt, stride=dil) if dil > 1 else slice(None)

            def window(ref, sref, base, n):
                if dil > 1:
                    return sref.at[pl.ds(base, n * dil)][pl.ds(r, n, stride=dil), :].astype(BF16)
                return ref[0, pl.ds(base, n), :]

            scs = [lax.dot_general(window(q_ref, qs_ref, qbases[g], qt), window(k_ref, ks_ref, kbases[g], kt),
                                   NT_DIMS, preferred_element_type=F32) for g in range(G)]
            ps, mns, lns = [], [], []
            for g in range(G):
                sc = jnp.where(jnp.abs(dcol + (w0s[g] - js[g] * qt)) <= span, scs[g], NEG)
                mx = jnp.max(sc, axis=-1, keepdims=True)
                p = jnp.exp(sc - mx)
                lns.append(jnp.broadcast_to(jnp.sum(p, axis=-1, keepdims=True), (qt, HEAD)))
                mns.append(jnp.broadcast_to(mx, (qt, HEAD)))
                ps.append(p.astype(BF16))
            ons = [jnp.dot(ps[g], window(v_ref, vs_ref, kbases[g], kt), preferred_element_type=F32)
                   for g in range(G)]
            for g in range(G):
                mv = m_ref.at[pl.ds(qbases[g], qt * dil)]
                lv = l_ref.at[pl.ds(qbases[g], qt * dil)]
                av = acc_ref.at[pl.ds(qbases[g], qt * dil)]
                if first:
                    mv[rows, :] = mns[g]
                    lv[rows, :] = lns[g]
                    av[rows, :] = ons[g]
                else:
                    mo = mv[rows, :]
                    mm = jnp.maximum(mo, mns[g])
                    wo = jnp.exp(mo - mm)
                    wn = jnp.exp(mns[g] - mm)
                    mv[rows, :] = mm
                    lv[rows, :] = wo * lv[rows, :] + wn * lns[g]
                    av[rows, :] = wo * av[rows, :] + wn * ons[g]
            return carry

        lax.fori_loop(0, ntile // G, tiles, 0)


def _attn_kernel(q_ref, k_ref, v_ref, o_ref, qs_ref, ks_ref, vs_ref, acc_ref, m_ref, l_ref):
    g = pl.program_id(2)
    for gi, (win, dil) in enumerate(ATTN_GROUPS):
        @pl.when(g == gi)
        def _(gi=gi, win=win, dil=dil):
            _attn_group(q_ref, k_ref, v_ref, qs_ref, ks_ref, vs_ref, acc_ref, m_ref, l_ref,
                        dil=dil, span=(win // 2) // dil, first=(gi == 0))

    @pl.when(g == len(ATTN_GROUPS) - 1)
    def _():
        o_ref[0] = (acc_ref[...] * (1.0 / l_ref[...])).astype(o_ref.dtype)


def _dilated_attention(a_rot, a_plain):
    bsz, s, _ = a_rot.shape
    ng = len(ATTN_GROUPS)
    vbase = HG_HEADS
    blk = (1, s, HEAD)
    return pl.pallas_call(
        _attn_kernel,
        grid=(bsz, ATTN_HEADS, ng),
        in_specs=[pl.BlockSpec(blk, lambda b, h, g: (b, 0, 2 * g * ATTN_HEADS + h)),
                  pl.BlockSpec(blk, lambda b, h, g: (b, 0, (2 * g + 1) * ATTN_HEADS + h)),
                  pl.BlockSpec(blk, lambda b, h, g: (b, 0, vbase + g * ATTN_HEADS + h))],
        out_specs=pl.BlockSpec(blk, lambda b, h, g: (b, 0, h)),
        out_shape=jax.ShapeDtypeStruct((bsz, s, ATTN_WIDTH), BF16),
        scratch_shapes=[pltpu.VMEM((s, HEAD), F32)] * 6,
        compiler_params=_cparams(("parallel", "parallel", "arbitrary")),
        name="dilated_attn",
    )(a_rot, a_rot, a_plain)


def _merge_kernel(*refs, bounds):
    nsrc = len(bounds) - 1
    hg_ref, attn_ref, gate_ref, who_ref, wao_ref, wout_ref, out_ref = refs[nsrc:]
    tm, d = out_ref.shape
    sub = min(ROW_SUBTILE, tm)
    for r0 in range(0, tm, sub):
        sl = slice(r0, r0 + sub)
        x = _pick_source(pl.program_id(0), refs[:nsrc], bounds, sl)
        yh = jnp.dot(hg_ref[sl, :], who_ref[...], preferred_element_type=F32)
        ya = jnp.dot(attn_ref[sl, :], wao_ref[...], preferred_element_type=F32)
        merged = gate_ref[sl, :d].astype(F32) * yh + gate_ref[sl, d:].astype(F32) * ya
        out_ref[sl, :] = x + jnp.dot(merged.astype(BF16), wout_ref[...], preferred_element_type=F32)


def _merge(srcs, hg, attn, gates, who, wao, wout, tm=512):
    d = srcs[0].shape[1]
    xspecs, bounds = _row_sources(srcs, tm)
    row = lambda w: pl.BlockSpec((tm, w), lambda i: (i, 0))
    full = lambda a: pl.BlockSpec(a.shape, lambda i: (0, 0))
    return pl.pallas_call(
        functools.partial(_merge_kernel, bounds=bounds),
        grid=(bounds[-1],),
        in_specs=xspecs + [row(d), row(ATTN_WIDTH), row(2 * d), full(who), full(wao), full(wout)],
        out_specs=row(d),
        out_shape=jax.ShapeDtypeStruct((bounds[-1] * tm, d), F32),
        compiler_params=_cparams(("parallel",)),
        name="merge",
    )(*srcs, hg, attn, gates, who, wao, wout)


def _xattn_kernel(h_ref, g_ref, wq_ref, kv_ref, wo_ref, out_ref):
    tm, d = h_ref.shape[1:]
    hd = d // XA_HEADS
    sub = min(ROW_SUBTILE, tm)
    for r0 in range(0, tm, sub):
        sl = slice(r0, r0 + sub)
        hx = h_ref[0, sl, :]
        u = (hx * _rms_scale(hx) * g_ref[...]).astype(BF16)
        q = (jnp.dot(u, wq_ref[...], preferred_element_type=F32) * (hd ** -0.5)).astype(BF16)
        outs = []
        for h in range(XA_HEADS):
            kh = kv_ref[0, :, h * hd:(h + 1) * hd]
            vh = kv_ref[0, :, d + h * hd:d + (h + 1) * hd]
            s = lax.dot_general(q[:, h * hd:(h + 1) * hd], kh, NT_DIMS, preferred_element_type=F32)
            p = jnp.exp(s - jnp.max(s, axis=-1, keepdims=True))
            den = jnp.sum(p, axis=-1, keepdims=True)
            outs.append(jnp.dot(p.astype(BF16), vh, preferred_element_type=F32) * (1.0 / den))
        o = jnp.concatenate(outs, axis=1).astype(BF16)
        out_ref[0, sl, :] = hx + jnp.dot(o, wo_ref[...], preferred_element_type=F32)


def _xattn(h3d, g, wq, kv, wo, tm=512):
    bsz, s, d = h3d.shape
    tm = min(tm, s)
    full = lambda a: pl.BlockSpec(a.shape, lambda b, i: (0, 0))
    return pl.pallas_call(
        _xattn_kernel,
        grid=(bsz, s // tm),
        in_specs=[pl.BlockSpec((1, tm, d), lambda b, i: (b, i, 0)),
                  pl.BlockSpec((1, d), lambda b, i: (0, 0)),
                  full(wq),
                  pl.BlockSpec((1,) + kv.shape[1:], lambda b, i: (b, 0, 0)),
                  full(wo)],
        out_specs=pl.BlockSpec((1, tm, d), lambda b, i: (b, i, 0)),
        out_shape=jax.ShapeDtypeStruct((bsz, s, d), F32),
        compiler_params=_cparams(("parallel", "parallel")),
        name="xattn",
    )(h3d, g.reshape(1, d), wq, kv, wo)


def _mlp_kernel(h_ref, g_ref, w1_ref, w2_ref, gf_ref, *out_refs, bounds):
    hx = h_ref[...]
    u = (hx * _rms_scale(hx) * g_ref[...]).astype(BF16)
    a = jnp.maximum(jnp.dot(u, w1_ref[...], preferred_element_type=F32), 0.0)
    a = (a * a).astype(BF16)
    y = hx + jnp.dot(a, w2_ref[...], preferred_element_type=F32)
    y = y * _rms_scale(y) * gf_ref[...]
    i = pl.program_id(0)
    for k, out_ref in enumerate(out_refs):
        @pl.when(jnp.logical_and(i >= bounds[k], i < bounds[k + 1]))
        def _(out_ref=out_ref):
            out_ref[...] = y


def _mlp(h2d, g, w1, w2, gf, group_rows, tm=512):
    m, d = h2d.shape
    assert sum(group_rows) == m
    bounds = [0]
    for r in group_rows:
        assert r % tm == 0
        bounds.append(bounds[-1] + r // tm)
    full = lambda a: pl.BlockSpec(a.shape, lambda i: (0, 0))
    out_specs = [pl.BlockSpec((tm, d), lambda i, lo=bounds[k], n=bounds[k + 1] - bounds[k]:
                              (jnp.clip(i - lo, 0, n - 1), 0)) for k in range(len(group_rows))]
    return pl.pallas_call(
        functools.partial(_mlp_kernel, bounds=bounds),
        grid=(m // tm,),
        in_specs=[pl.BlockSpec((tm, d), lambda i: (i, 0)), pl.BlockSpec((1, d), lambda i: (0, 0)),
                  full(w1), full(w2), pl.BlockSpec((1, d), lambda i: (0, 0))],
        out_specs=out_specs,
        out_shape=[jax.ShapeDtypeStruct((r, d), F32) for r in group_rows],
        compiler_params=_cparams(("arbitrary",)),
        name="mlp_final",
    )(h2d, g.reshape(1, d), w1, w2, gf.reshape(1, d))


def _rotary_tables(s):
    half = ROT_DIM // 2
    inv = ROPE_THETA ** (-jnp.arange(half, dtype=F32) * 2.0 / ROT_DIM)
    ang = jnp.arange(s, dtype=F32)[:, None] * inv[None, :]
    cos, sin = jnp.cos(ang), jnp.sin(ang)
    pad = jnp.zeros((s, HEAD - ROT_DIM), F32)
    zero = jnp.zeros((s, half), F32)
    cos_t = jnp.concatenate([cos, cos, pad + 1.0], axis=1)
    sin_lo = jnp.concatenate([-sin, zero, pad], axis=1)
    sin_hi = jnp.concatenate([zero, sin, pad], axis=1)
    return cos_t, sin_lo, sin_hi


def _encode(xs, mems, mix_norm_g, w_in, hgrn_lb_logits, hgrn_gnorm_g, w_hgrn_o, w_attn_o, w_out,
            xa_norm_g, mem_norm_g, w_xq, w_xkv, w_xo, ffn_norm_g, w_ffn1, w_ffn2, final_norm_g,
            hgrn_rows=2048):
    s, d = xs[0].shape[1:]
    bsz = sum(x.shape[0] for x in xs)
    t = bsz * s
    depth = w_in.shape[0]
    assert depth == 1, "the final norm is fused into the (single) layer's MLP call"
    l = 0
    fd = HG_HEADS * HEAD
    lb_all = jnp.cumsum(jax.nn.softmax(hgrn_lb_logits.astype(F32), axis=1), axis=1)
    sizes = (fd,) * 5 + (ATTN_WIDTH,) * 9 + (d, d)
    offs = np.concatenate([[0], np.cumsum(sizes)])
    cos_t, sin_lo, sin_hi = _rotary_tables(s)
    tm = min(TOKEN_TILE, s)
    tmp = min(PROJ_TOKEN_TILE, s)
    tabspec = pl.BlockSpec((tmp, HEAD), lambda i: (i % (s // tmp), 0))
    srcs = [x.reshape(-1, d) for x in xs]
    mem = jnp.concatenate(mems, axis=0) if len(mems) > 1 else mems[0]

    wl = w_in[l]
    seg = lambda p: wl[:, offs[p]:offs[p + 1]]
    bf = lambda a: a.astype(BF16)
    w_silu = bf(jnp.concatenate([seg(0), seg(4)], axis=1))
    w_lf = bf(jnp.concatenate([seg(1), seg(2)], axis=1))
    w_plain = bf(jnp.concatenate([seg(3), seg(7), seg(10), seg(13)], axis=1))
    w_rot = bf(jnp.concatenate([seg(5), seg(6), seg(8), seg(9), seg(11), seg(12)], axis=1))
    w_gate = bf(jnp.concatenate([seg(14), seg(15)], axis=1))
    g_mix = mix_norm_g[l]

    u = _prenorm(srcs, g_mix, tm)
    proj = functools.partial(_norm_proj, u, g_mix, tm=tmp, norm=False)
    a_silu = proj(w_silu, _ep_silu, BF16, name="proj_silu")
    lb_row = lb_all[:, l].reshape(1, 2 * fd)
    a_lf = proj(w_lf, _ep_logf, F32, extras=(lb_row,),
                extra_specs=(pl.BlockSpec((1, 2 * fd), lambda i: (0, 0)),), name="proj_logf")
    a_plain = proj(w_plain, _ep_plain, BF16, name="proj_plain")
    qscale = jnp.tile(jnp.concatenate([jnp.full((ATTN_WIDTH,), HEAD ** -0.5, F32),
                                       jnp.ones((ATTN_WIDTH,), F32)]), 3).reshape(1, 6 * ATTN_WIDTH)
    a_rot = proj(w_rot, _ep_rotary, BF16, extras=(cos_t, sin_lo, sin_hi, qscale),
                 extra_specs=(tabspec, tabspec, tabspec, pl.BlockSpec((1, 6 * ATTN_WIDTH), lambda i: (0, 0))),
                 name="proj_rotary")
    a_gate = proj(w_gate, _ep_sigmoid, BF16, name="proj_gate")

    a_silu3 = a_silu.reshape(bsz, s, 2 * fd)
    a_lf3 = a_lf.reshape(bsz, s, 2 * fd)
    a_plain3 = a_plain.reshape(bsz, s, -1)
    o_fwd = _hgrn_pass(a_silu3, a_lf3, a_plain3, hgrn_gnorm_g[l], None, False, hgrn_rows)
    hg = _hgrn_pass(a_silu3, a_lf3, a_plain3, hgrn_gnorm_g[l], o_fwd, True, hgrn_rows)

    attn = _dilated_attention(a_rot.reshape(bsz, s, -1), a_plain3)

    h2d = _merge(srcs, hg.reshape(t, fd), attn.reshape(t, ATTN_WIDTH), a_gate,
                 w_hgrn_o[l].astype(BF16), w_attn_o[l].astype(BF16), w_out[l].astype(BF16), tm=tm)

    nm = mem.shape[1]
    kv = _norm_proj(mem.reshape(bsz * nm, d), mem_norm_g[l], w_xkv[l].astype(BF16), _ep_plain, BF16,
                    tm=nm, name="proj_memkv")
    h3d = _xattn(h2d.reshape(bsz, s, d), xa_norm_g[l], w_xq[l].astype(BF16), kv.reshape(bsz, nm, 2 * d),
                 w_xo[l].astype(BF16), tm=tm)
    outs = _mlp(h3d.reshape(t, d), ffn_norm_g[l], w_ffn1[l].astype(BF16), w_ffn2[l].astype(BF16), final_norm_g,
                [x.shape[0] * s for x in xs], tm=tm)
    return [o.reshape(x.shape) for o, x in zip(outs, xs)]


def kernel(x_prompt, x_sample, mem_prompt, mem_sample, mix_norm_g, w_in, hgrn_lb_logits, hgrn_gnorm_g,
           w_hgrn_o, w_attn_o, w_out, xa_norm_g, mem_norm_g, w_xq, w_xkv, w_xo, ffn_norm_g, w_ffn1,
           w_ffn2, final_norm_g):
    assert x_prompt.shape[1:] == x_sample.shape[1:]
    y_prompt, y_sample = _encode(
        [x_prompt, x_sample], [mem_prompt, mem_sample], mix_norm_g, w_in, hgrn_lb_logits, hgrn_gnorm_g,
        w_hgrn_o, w_attn_o, w_out, xa_norm_g, mem_norm_g, w_xq, w_xkv, w_xo, ffn_norm_g, w_ffn1, w_ffn2,
        final_norm_g)
    return y_prompt, y_sample
```

```python
import functools

import numpy as np
import jax
import jax.numpy as jnp
from jax import lax
from jax.experimental import pallas as pl
from jax.experimental.pallas import tpu as pltpu

F32 = jnp.float32
BF16 = jnp.bfloat16

RMS_EPS = 1e-6
ROPE_THETA = 500000.0
HG_HEADS = 8
HEAD = 128
ATTN_HEADS = 4
ATTN_WIDTH = ATTN_HEADS * HEAD
ATTN_GROUPS = ((128, 1), (512, 4), (2048, 16))
ROT_DIM = HEAD // 4
XA_HEADS = 4
CHUNK = 64
SUB = 8
NEG = -1e30
HG_PACK = 2
SAFE_CHUNK_LOG_DECAY = -60.0
HGRN_UNROLL = 4
ATTN_UNROLL = 4
NT_DIMS = (((1,), (1,)), ((), ()))
TN_DIMS = (((0,), (0,)), ((), ()))

TOKEN_TILE = 512
PROJ_TOKEN_TILE = 1024

VMEM_LIMIT = 56 * 1024 * 1024


def _cparams(sem):
    return pltpu.CompilerParams(dimension_semantics=sem, vmem_limit_bytes=VMEM_LIMIT)


def _sigmoid(x):
    return 1.0 / (1.0 + jnp.exp(-x))


def _rms_scale(xf):
    return lax.rsqrt(jnp.mean(xf * xf, axis=-1, keepdims=True) + RMS_EPS)


def _ep_plain(acc):
    return acc


def _ep_silu(acc):
    return acc * _sigmoid(acc)


def _ep_sigmoid(acc):
    return _sigmoid(acc)


def _ep_logf(acc, lb):
    return jnp.log(lb + (1.0 - lb) * _sigmoid(acc))


def _ep_rotary(acc, cos, sin_lo, sin_hi, colscale):
    half = ROT_DIM // 2
    outs = []
    for c in range(acc.shape[1] // HEAD):
        t = acc[:, c * HEAD:(c + 1) * HEAD]
        r = t * cos + pltpu.roll(t, HEAD - half, 1) * sin_lo + pltpu.roll(t, half, 1) * sin_hi
        outs.append(r)
    return jnp.concatenate(outs, axis=1) * colscale


def _row_sources(srcs, tm):
    bounds = [0]
    for a in srcs:
        assert a.shape[0] % tm == 0
        bounds.append(bounds[-1] + a.shape[0] // tm)
    specs = [pl.BlockSpec((tm, a.shape[1]),
                          lambda i, lo=bounds[k], n=bounds[k + 1] - bounds[k]: (jnp.clip(i - lo, 0, n - 1), 0))
             for k, a in enumerate(srcs)]
    return specs, bounds


def _pick_source(i, refs, bounds):
    x = refs[-1][...]
    for k in range(len(refs) - 2, -1, -1):
        x = jnp.where(i < bounds[k + 1], refs[k][...], x)
    return x


def _first_proj_kernel(*refs, epilogue, bounds):
    nsrc = len(bounds) - 1
    g_ref, w_ref, o_ref, u_ref = refs[nsrc:]
    xf = _pick_source(pl.program_id(0), refs[:nsrc], bounds)
    u = (xf * _rms_scale(xf) * g_ref[...]).astype(BF16)
    u_ref[...] = u
    o_ref[...] = epilogue(jnp.dot(u, w_ref[...], preferred_element_type=F32)).astype(o_ref.dtype)


def _first_proj(srcs, g, w, epilogue, out_dtype, tm, name):
    d = srcs[0].shape[1]
    n = w.shape[1]
    specs, bounds = _row_sources(srcs, tm)
    m = bounds[-1] * tm
    return pl.pallas_call(
        functools.partial(_first_proj_kernel, epilogue=epilogue, bounds=bounds),
        grid=(bounds[-1],),
        in_specs=specs + [pl.BlockSpec((1, d), lambda i: (0, 0)), pl.BlockSpec((d, n), lambda i: (0, 0))],
        out_specs=[pl.BlockSpec((tm, n), lambda i: (i, 0)), pl.BlockSpec((tm, d), lambda i: (i, 0))],
        out_shape=[jax.ShapeDtypeStruct((m, n), out_dtype), jax.ShapeDtypeStruct((m, d), BF16)],
        compiler_params=_cparams(("parallel",)),
        name=name,
    )(*srcs, g.reshape(1, d), w)


def _norm_proj_kernel(*refs, epilogue, n_extra, norm):
    x_ref, g_ref, w_ref = refs[:3]
    extra = refs[3:3 + n_extra]
    o_ref = refs[3 + n_extra]
    if norm:
        xf = x_ref[...]
        u = (xf * _rms_scale(xf) * g_ref[...]).astype(BF16)
    else:
        u = x_ref[...]
    acc = jnp.dot(u, w_ref[...], preferred_element_type=F32)
    o_ref[...] = epilogue(acc, *[e[...] for e in extra]).astype(o_ref.dtype)


def _norm_proj(x2d, g, w, epilogue, out_dtype, extras=(), extra_specs=(), tm=512, name="norm_proj", norm=True):
    m, d = x2d.shape
    n = w.shape[1]
    tm = min(tm, m)
    assert m % tm == 0
    in_specs = [
        pl.BlockSpec((tm, d), lambda i: (i, 0)),
        pl.BlockSpec((1, d), lambda i: (0, 0)),
        pl.BlockSpec((d, n), lambda i: (0, 0)),
    ] + list(extra_specs)
    return pl.pallas_call(
        functools.partial(_norm_proj_kernel, epilogue=epilogue, n_extra=len(extras), norm=norm),
        grid=(m // tm,),
        in_specs=in_specs,
        out_specs=pl.BlockSpec((tm, n), lambda i: (i, 0)),
        out_shape=jax.ShapeDtypeStruct((m, n), out_dtype),
        compiler_params=_cparams(("parallel",)),
        name=name,
    )(x2d, g.reshape(1, d), w, *extras)


def _hgrn_robust_chunk(r0, ls, q_ref, lf_ref, v_ref, tri_ref, st_ref, kc_ref, bc_ref, rev):
    C = CHUNK
    ng = C // SUB
    rowl = lax.broadcasted_iota(jnp.int32, (C, HEAD), 0)
    sub_pos = rowl & (SUB - 1)
    arow = lax.broadcasted_iota(jnp.int32, (C, C), 0)
    acol = lax.broadcasted_iota(jnp.int32, (C, C), 1)
    ones_b = jnp.ones((HEAD, HEAD), BF16)
    zeros_g = jnp.zeros((SUB, HEAD), F32)

    lf = lf_ref[0, pl.ds(r0, C), ls]
    q = q_ref[0, pl.ds(r0, C), ls].astype(F32)
    v = v_ref[0, pl.ds(r0, C), ls]
    k = 1.0 - jnp.exp(lf)
    hi = lf.astype(BF16)
    mid = (lf - hi.astype(F32)).astype(BF16)
    b = jnp.dot(tri_ref[...], jnp.concatenate([hi, mid], axis=0), preferred_element_type=F32)
    kc_ref[pl.ds(SUB, C), :] = k
    bc_ref[pl.ds(SUB, C), :] = b
    b_end = bc_ref[pl.ds(SUB + (0 if rev else C - 1), 1), :]

    st = st_ref[:, ls]
    qi = (q * jnp.exp(b)).astype(BF16)
    o = lax.dot_general(qi, st.astype(BF16), NT_DIMS, preferred_element_type=F32)
    kl = (k * jnp.exp(b_end - b)).astype(BF16)
    st_ref[:, ls] = st * jnp.exp(b_end) + lax.dot_general(v, kl, TN_DIMS, preferred_element_type=F32)

    a = jnp.zeros((C, C), F32)
    h = C // 2
    while h >= SUB:
        qparts, kparts = [], []
        for gi in range(ng):
            t0 = gi * SUB
            blk = t0 // (2 * h)
            in_upper = (t0 % (2 * h)) >= h
            is_query = (not in_upper) if rev else in_upper
            rr = blk * 2 * h + (h if rev else h - 1)
            bref = bc_ref[pl.ds(SUB + rr, 1), :]
            bg = b[t0:t0 + SUB]
            if is_query:
                qparts.append(q[t0:t0 + SUB] * jnp.exp(bg - bref))
                kparts.append(zeros_g)
            else:
                qparts.append(zeros_g)
                kparts.append(k[t0:t0 + SUB] * jnp.exp(bref - bg))
        qh = jnp.concatenate(qparts, axis=0).astype(BF16)
        kh = jnp.concatenate(kparts, axis=0).astype(BF16)
        ah = lax.dot_general(qh, kh, NT_DIMS, preferred_element_type=F32)
        if 2 * h < C:
            ah = jnp.where((arow ^ acol) < 2 * h, ah, 0.0)
        a = a + ah
        h //= 2

    for d in range(SUB):
        sh = SUB + (d if rev else -d)
        ks = kc_ref[pl.ds(sh, C), :]
        bs = bc_ref[pl.ds(sh, C), :]
        ok = (sub_pos + d <= SUB - 1) if rev else (sub_pos >= d)
        p = jnp.where(ok, q * ks * jnp.exp(b - bs), 0.0).astype(BF16)
        rs = jnp.dot(p, ones_b, preferred_element_type=F32)[:, :C]
        tgt = (arow + d) if rev else (arow - d)
        a = a + jnp.where(acol == tgt, rs, 0.0)

    return o + jnp.dot(a.astype(BF16), v, preferred_element_type=F32)


def _blockdiag2(x):
    z = jnp.zeros((x.shape[0], HEAD), x.dtype)
    return jnp.concatenate([jnp.concatenate([x[:, :HEAD], z], axis=1),
                            jnp.concatenate([z, x[:, HEAD:]], axis=1)], axis=0)


def _hgrn_kernel(*refs, rev, final, nchunk):
    if final:
        (q_ref, lf_ref, v_ref, tri_ref, sel_ref, of_ref, g_ref, gn_ref, o_ref,
         st_ref, qi_ref, a_ref, upd_ref, snap_ref, dec_ref, b_ref, kc_ref, bc_ref) = refs
    else:
        (q_ref, lf_ref, v_ref, tri_ref, sel_ref, o_ref,
         st_ref, qi_ref, a_ref, upd_ref, snap_ref, dec_ref, b_ref, kc_ref, bc_ref) = refs
        of_ref = g_ref = gn_ref = None
    C = CHUNK
    W = HG_PACK * HEAD

    @pl.when(pl.program_id(2) == 0)
    def _():
        st_ref[...] = jnp.zeros_like(st_ref)
        kc_ref[...] = jnp.zeros_like(kc_ref)
        bc_ref[...] = jnp.zeros_like(bc_ref)

    def emit(o, r0, ls):
        if final:
            tot = o + of_ref[0, pl.ds(r0, C), ls]
            parts = []
            for j in range(tot.shape[1] // HEAD):
                tj = tot[:, j * HEAD:(j + 1) * HEAD]
                parts.append(tj * _rms_scale(tj) * gn_ref[...])
            y = parts[0] if len(parts) == 1 else jnp.concatenate(parts, axis=1)
            y = y * g_ref[0, pl.ds(r0, C), ls].astype(F32)
            o_ref[0, pl.ds(r0, C), ls] = y.astype(o_ref.dtype)
        else:
            o_ref[0, pl.ds(r0, C), ls] = o

    sums = jnp.dot(sel_ref[...], lf_ref[0].astype(BF16), preferred_element_type=F32)
    safe = jnp.min(sums) >= SAFE_CHUNK_LOG_DECAY

    @pl.when(safe)
    def _():
        arow = lax.broadcasted_iota(jnp.int32, (C, 2 * C), 0)
        acol = lax.broadcasted_iota(jnp.int32, (C, 2 * C), 1) & (C - 1)
        causal = (acol >= arow) if rev else (acol <= arow)

        G = HGRN_UNROLL
        assert nchunk % G == 0

        G2 = 2 * G
        assert nchunk % G2 == 0

        def phase0(i, carry):
            r0s = [pl.multiple_of((i * G2 + g) * C, C) for g in range(G2)]
            lfs = [lf_ref[0, pl.ds(r0, C), :] for r0 in r0s]
            his = [lf.astype(BF16) for lf in lfs]
            mids = [(lf - hi.astype(F32)).astype(BF16) for lf, hi in zip(lfs, his)]
            rhs = jnp.concatenate([jnp.concatenate(his, axis=1), jnp.concatenate(mids, axis=1)], axis=0)
            ball = jnp.dot(tri_ref[...], rhs, preferred_element_type=F32)
            for g in range(G2):
                b_ref[pl.ds(r0s[g], C), :] = ball[:, g * W:(g + 1) * W]
            return carry

        lax.fori_loop(0, nchunk // G2, phase0, 0)

        def phase1(i, carry):
            cs = [i * G + g for g in range(G)]
            r0s = [pl.multiple_of(c * C, C) for c in cs]
            lfs = [lf_ref[0, pl.ds(r0, C), :] for r0 in r0s]
            qis, kls, kbars = [], [], []
            for g in range(G):
                b = b_ref[pl.ds(r0s[g], C), :]
                k = 1.0 - jnp.exp(lfs[g])
                q = q_ref[0, pl.ds(r0s[g], C), :].astype(F32)
                qi = (q * jnp.exp(b)).astype(BF16)
                qi_ref[pl.ds(r0s[g], C), :] = qi
                b_end = b[0:1] if rev else b[C - 1:C]
                dec_ref[pl.ds(pl.multiple_of(cs[g] * SUB, SUB), SUB), :] = jnp.broadcast_to(jnp.exp(b_end), (SUB, W))
                qis.append(qi)
                kls.append((k * jnp.exp(b_end - b)).astype(BF16))
                kbars.append((k * jnp.exp(-b)).astype(BF16))
            avals = [lax.dot_general(qis[g], _blockdiag2(kbars[g]), NT_DIMS, preferred_element_type=F32)
                     for g in range(G)]
            for g in range(G):
                v = v_ref[0, pl.ds(r0s[g], C), :]
                for j in range(HG_PACK):
                    ls = slice(j * HEAD, (j + 1) * HEAD)
                    upd_ref[pl.ds(pl.multiple_of(cs[g] * HEAD, HEAD), HEAD), ls] = lax.dot_general(
                        v[:, ls], kls[g][:, ls], TN_DIMS, preferred_element_type=F32)
            for g in range(G):
                a_ref[pl.ds(r0s[g], C), :] = jnp.where(causal, avals[g], 0.0).astype(BF16)
            return carry

        lax.fori_loop(0, nchunk // G, phase1, 0)

        def phase2(i, carry):
            c = (nchunk - 1 - i) if rev else i
            s0 = pl.multiple_of(c * HEAD, HEAD)
            st = st_ref[...]
            snap_ref[pl.ds(s0, HEAD), :] = st.astype(BF16)
            dec = dec_ref[pl.ds(pl.multiple_of(c * SUB, SUB), 1), :]
            st_ref[...] = st * dec + upd_ref[pl.ds(s0, HEAD), :]
            return carry

        lax.fori_loop(0, nchunk, phase2, 0, unroll=G)

        def phase3(i, carry):
            cs = [i * G + g for g in range(G)]
            r0s = [pl.multiple_of(c * C, C) for c in cs]
            snaps = [snap_ref[pl.ds(pl.multiple_of(c * HEAD, HEAD), HEAD), :] for c in cs]
            o1 = [lax.dot_general(qi_ref[pl.ds(r0s[g], C), :], _blockdiag2(snaps[g]), NT_DIMS,
                                  preferred_element_type=F32) for g in range(G)]
            o2 = [jnp.dot(a_ref[pl.ds(r0s[g], C), :], _blockdiag2(v_ref[0, pl.ds(r0s[g], C), :]),
                          preferred_element_type=F32) for g in range(G)]
            for g in range(G):
                emit(o1[g] + o2[g], r0s[g], slice(None))
            return carry

        lax.fori_loop(0, nchunk // G, phase3, 0)

    @pl.when(jnp.logical_not(safe))
    def _():
        def body(i, carry):
            c = (nchunk - 1 - i) if rev else i
            r0 = pl.multiple_of(c * C, C)
            for j in range(HG_PACK):
                ls = slice(j * HEAD, (j + 1) * HEAD)
                o = _hgrn_robust_chunk(r0, ls, q_ref, lf_ref, v_ref, tri_ref, st_ref, kc_ref, bc_ref, rev)
                emit(o, r0, ls)
            return carry

        lax.fori_loop(0, nchunk, body, 0)


def _tri_matrix(rev):
    t = np.arange(CHUNK)
    m = (t[None, :] >= t[:, None]) if rev else (t[None, :] <= t[:, None])
    return jnp.asarray(np.concatenate([m, m], axis=1).astype(np.float32), dtype=BF16)


def _chunk_sum_matrix(rows):
    nchunk = rows // CHUNK
    nsel = -(-nchunk // 16) * 16
    m = np.zeros((nsel, rows), np.float32)
    m[np.arange(rows) // CHUNK, np.arange(rows)] = 1.0
    return jnp.asarray(m, dtype=BF16)


def _hgrn_pass(a_silu, a_lf, a_plain, gnorm, o_fwd, rev, rows):
    bsz, s, _ = a_silu.shape
    rows = min(rows, s)
    nblk = s // rows
    nchunk = rows // CHUNK
    w = HG_PACK * HEAD
    npk = HG_HEADS // HG_PACK
    final = o_fwd is not None
    seq = (lambda i: nblk - 1 - i) if rev else (lambda i: i)
    fcol = npk if rev else 0
    sel = _chunk_sum_matrix(rows)
    in_specs = [
        pl.BlockSpec((1, rows, w), lambda b, h, i: (b, seq(i), h)),
        pl.BlockSpec((1, rows, w), lambda b, h, i: (b, seq(i), h + fcol)),
        pl.BlockSpec((1, rows, w), lambda b, h, i: (b, seq(i), h)),
        pl.BlockSpec((CHUNK, 2 * CHUNK), lambda b, h, i: (0, 0)),
        pl.BlockSpec(sel.shape, lambda b, h, i: (0, 0)),
    ]
    args = [a_silu, a_lf, a_plain, _tri_matrix(rev), sel]
    if final:
        in_specs += [
            pl.BlockSpec((1, rows, w), lambda b, h, i: (b, seq(i), h)),
            pl.BlockSpec((1, rows, w), lambda b, h, i: (b, seq(i), h + npk)),
            pl.BlockSpec((1, HEAD), lambda b, h, i: (0, 0)),
        ]
        args += [o_fwd, a_silu, gnorm.reshape(1, HEAD)]
    return pl.pallas_call(
        functools.partial(_hgrn_kernel, rev=rev, final=final, nchunk=nchunk),
        grid=(bsz, npk, nblk),
        in_specs=in_specs,
        out_specs=pl.BlockSpec((1, rows, w), lambda b, h, i: (b, seq(i), h)),
        out_shape=jax.ShapeDtypeStruct((bsz, s, HG_HEADS * HEAD), BF16 if final else F32),
        scratch_shapes=[
            pltpu.VMEM((HEAD, w), F32),
            pltpu.VMEM((rows, w), BF16),
            pltpu.VMEM((rows, 2 * CHUNK), BF16),
            pltpu.VMEM((nchunk * HEAD, w), F32),
            pltpu.VMEM((nchunk * HEAD, w), BF16),
            pltpu.VMEM((nchunk * SUB, w), F32),
            pltpu.VMEM((rows, w), F32),
            pltpu.VMEM((CHUNK + 2 * SUB, HEAD), F32),
            pltpu.VMEM((CHUNK + 2 * SUB, HEAD), F32),
        ],
        compiler_params=_cparams(("parallel", "parallel", "arbitrary")),
        name="hgrn_bwd" if rev else "hgrn_fwd",
    )(*args)


def _attn_group(q_ref, k_ref, v_ref, qs_ref, ks_ref, vs_ref, acc_ref, m_ref, l_ref, *, dil, span, first):
    s = q_ref.shape[1]
    length = s // dil
    qt, kt = 2 * span, 4 * span
    ntile = length // qt
    dcol = (lax.broadcasted_iota(jnp.int32, (qt, kt), 1) - lax.broadcasted_iota(jnp.int32, (qt, kt), 0))
    if dil > 1:
        qs_ref[...] = q_ref[0].astype(F32)
        ks_ref[...] = k_ref[0].astype(F32)
        vs_ref[...] = v_ref[0].astype(F32)

    G = min(ATTN_UNROLL, ntile)
    assert ntile % G == 0
    for r in range(dil):
        def tiles(i, carry):
            js = [i * G + g for g in range(G)]
            w0s = [jnp.clip(j * qt - span, 0, length - kt) for j in js]
            qbases = [pl.multiple_of(j * (qt * dil), qt * dil) for j in js]
            kbases = [pl.multiple_of(w0 * dil, span * dil) for w0 in w0s]
            rows = pl.ds(r, qt, stride=dil) if dil > 1 else slice(None)

            def window(ref, sref, base, n):
                if dil > 1:
                    return sref.at[pl.ds(base, n * dil)][pl.ds(r, n, stride=dil), :].astype(BF16)
                return ref[0, pl.ds(base, n), :]

            scs = [lax.dot_general(window(q_ref, qs_ref, qbases[g], qt), window(k_ref, ks_ref, kbases[g], kt),
                                   NT_DIMS, preferred_element_type=F32) for g in range(G)]
            ps, mns, lns = [], [], []
            for g in range(G):
                sc = jnp.where(jnp.abs(dcol + (w0s[g] - js[g] * qt)) <= span, scs[g], NEG)
                mx = jnp.max(sc, axis=-1, keepdims=True)
                p = jnp.exp(sc - mx)
                lns.append(jnp.broadcast_to(jnp.sum(p, axis=-1, keepdims=True), (qt, HEAD)))
                mns.append(jnp.broadcast_to(mx, (qt, HEAD)))
                ps.append(p.astype(BF16))
            ons = [jnp.dot(ps[g], window(v_ref, vs_ref, kbases[g], kt), preferred_element_type=F32)
                   for g in range(G)]
            for g in range(G):
                mv = m_ref.at[pl.ds(qbases[g], qt * dil)]
                lv = l_ref.at[pl.ds(qbases[g], qt * dil)]
                av = acc_ref.at[pl.ds(qbases[g], qt * dil)]
                if first:
                    mv[rows, :] = mns[g]
                    lv[rows, :] = lns[g]
                    av[rows, :] = ons[g]
                else:
                    mo = mv[rows, :]
                    mm = jnp.maximum(mo, mns[g])
                    wo = jnp.exp(mo - mm)
                    wn = jnp.exp(mns[g] - mm)
                    mv[rows, :] = mm
                    lv[rows, :] = wo * lv[rows, :] + wn * lns[g]
                    av[rows, :] = wo * av[rows, :] + wn * ons[g]
            return carry

        lax.fori_loop(0, ntile // G, tiles, 0)


def _attn_kernel(q_ref, k_ref, v_ref, o_ref, qs_ref, ks_ref, vs_ref, acc_ref, m_ref, l_ref):
    g = pl.program_id(2)
    for gi, (win, dil) in enumerate(ATTN_GROUPS):
        @pl.when(g == gi)
        def _(gi=gi, win=win, dil=dil):
            _attn_group(q_ref, k_ref, v_ref, qs_ref, ks_ref, vs_ref, acc_ref, m_ref, l_ref,
                        dil=dil, span=(win // 2) // dil, first=(gi == 0))

    @pl.when(g == len(ATTN_GROUPS) - 1)
    def _():
        o_ref[0] = (acc_ref[...] * (1.0 / l_ref[...])).astype(o_ref.dtype)


def _dilated_attention(a_rot, a_plain):
    bsz, s, _ = a_rot.shape
    ng = len(ATTN_GROUPS)
    vbase = HG_HEADS
    blk = (1, s, HEAD)
    return pl.pallas_call(
        _attn_kernel,
        grid=(bsz, ATTN_HEADS, ng),
        in_specs=[pl.BlockSpec(blk, lambda b, h, g: (b, 0, 2 * g * ATTN_HEADS + h)),
                  pl.BlockSpec(blk, lambda b, h, g: (b, 0, (2 * g + 1) * ATTN_HEADS + h)),
                  pl.BlockSpec(blk, lambda b, h, g: (b, 0, vbase + g * ATTN_HEADS + h))],
        out_specs=pl.BlockSpec(blk, lambda b, h, g: (b, 0, h)),
        out_shape=jax.ShapeDtypeStruct((bsz, s, ATTN_WIDTH), BF16),
        scratch_shapes=[pltpu.VMEM((s, HEAD), F32)] * 6,
        compiler_params=_cparams(("parallel", "parallel", "arbitrary")),
        name="dilated_attn",
    )(a_rot, a_rot, a_plain)


def _merge_xattn_kernel(*refs, bounds):
    nsrc = len(bounds) - 1
    (hg_ref, attn_ref, gate_ref, who_ref, wao_ref, wout_ref,
     g_ref, wq_ref, kv_ref, wo_ref, out_ref) = refs[nsrc:]
    x = _pick_source(pl.program_id(0), refs[:nsrc], bounds)
    d = x.shape[1]
    hd = d // XA_HEADS
    yh = jnp.dot(hg_ref[...], who_ref[...], preferred_element_type=F32)
    ya = jnp.dot(attn_ref[...], wao_ref[...], preferred_element_type=F32)
    merged = gate_ref[:, :d].astype(F32) * yh + gate_ref[:, d:].astype(F32) * ya
    hx = x + jnp.dot(merged.astype(BF16), wout_ref[...], preferred_element_type=F32)
    u = (hx * _rms_scale(hx) * g_ref[...]).astype(BF16)
    q = (jnp.dot(u, wq_ref[...], preferred_element_type=F32) * (hd ** -0.5)).astype(BF16)
    outs = []
    for h in range(XA_HEADS):
        kh = kv_ref[0, :, h * hd:(h + 1) * hd]
        vh = kv_ref[0, :, d + h * hd:d + (h + 1) * hd]
        s = lax.dot_general(q[:, h * hd:(h + 1) * hd], kh, NT_DIMS, preferred_element_type=F32)
        p = jnp.exp(s - jnp.max(s, axis=-1, keepdims=True))
        den = jnp.sum(p, axis=-1, keepdims=True)
        outs.append(jnp.dot(p.astype(BF16), vh, preferred_element_type=F32) * (1.0 / den))
    o = jnp.concatenate(outs, axis=1).astype(BF16)
    out_ref[...] = hx + jnp.dot(o, wo_ref[...], preferred_element_type=F32)


def _merge_xattn(srcs, hg, attn, gates, who, wao, wout, g, wq, kv, wo, seq_len, tm=512):
    d = srcs[0].shape[1]
    xspecs, bounds = _row_sources(srcs, tm)
    tiles_per_seq = seq_len // tm
    row = lambda w: pl.BlockSpec((tm, w), lambda i: (i, 0))
    full = lambda a: pl.BlockSpec(a.shape, lambda i: (0, 0))
    return pl.pallas_call(
        functools.partial(_merge_xattn_kernel, bounds=bounds),
        grid=(bounds[-1],),
        in_specs=xspecs + [row(d), row(ATTN_WIDTH), row(2 * d), full(who), full(wao), full(wout),
                           pl.BlockSpec((1, d), lambda i: (0, 0)), full(wq),
                           pl.BlockSpec((1,) + kv.shape[1:], lambda i: (i // tiles_per_seq, 0, 0)), full(wo)],
        out_specs=row(d),
        out_shape=jax.ShapeDtypeStruct((bounds[-1] * tm, d), F32),
        compiler_params=_cparams(("parallel",)),
        name="merge_xattn",
    )(*srcs, hg, attn, gates, who, wao, wout, g.reshape(1, d), wq, kv, wo)


def _mlp_kernel(h_ref, g_ref, w1_ref, w2_ref, gf_ref, *out_refs, bounds):
    hx = h_ref[...]
    u = (hx * _rms_scale(hx) * g_ref[...]).astype(BF16)
    a = jnp.maximum(jnp.dot(u, w1_ref[...], preferred_element_type=F32), 0.0)
    a = (a * a).astype(BF16)
    y = hx + jnp.dot(a, w2_ref[...], preferred_element_type=F32)
    y = y * _rms_scale(y) * gf_ref[...]
    i = pl.program_id(0)
    for k, out_ref in enumerate(out_refs):
        @pl.when(jnp.logical_and(i >= bounds[k], i < bounds[k + 1]))
        def _(out_ref=out_ref):
            out_ref[...] = y


def _mlp(h2d, g, w1, w2, gf, group_rows, tm=512):
    m, d = h2d.shape
    assert sum(group_rows) == m
    bounds = [0]
    for r in group_rows:
        assert r % tm == 0
        bounds.append(bounds[-1] + r // tm)
    full = lambda a: pl.BlockSpec(a.shape, lambda i: (0, 0))
    out_specs = [pl.BlockSpec((tm, d), lambda i, lo=bounds[k], n=bounds[k + 1] - bounds[k]:
                              (jnp.clip(i - lo, 0, n - 1), 0)) for k in range(len(group_rows))]
    return pl.pallas_call(
        functools.partial(_mlp_kernel, bounds=bounds),
        grid=(m // tm,),
        in_specs=[pl.BlockSpec((tm, d), lambda i: (i, 0)), pl.BlockSpec((1, d), lambda i: (0, 0)),
                  full(w1), full(w2), pl.BlockSpec((1, d), lambda i: (0, 0))],
        out_specs=out_specs,
        out_shape=[jax.ShapeDtypeStruct((r, d), F32) for r in group_rows],
        compiler_params=_cparams(("arbitrary",)),
        name="mlp_final",
    )(h2d, g.reshape(1, d), w1, w2, gf.reshape(1, d))


def _rotary_tables(s):
    half = ROT_DIM // 2
    inv = ROPE_THETA ** (-jnp.arange(half, dtype=F32) * 2.0 / ROT_DIM)
    ang = jnp.arange(s, dtype=F32)[:, None] * inv[None, :]
    cos, sin = jnp.cos(ang), jnp.sin(ang)
    pad = jnp.zeros((s, HEAD - ROT_DIM), F32)
    zero = jnp.zeros((s, half), F32)
    cos_t = jnp.concatenate([cos, cos, pad + 1.0], axis=1)
    sin_lo = jnp.concatenate([-sin, zero, pad], axis=1)
    sin_hi = jnp.concatenate([zero, sin, pad], axis=1)
    return cos_t, sin_lo, sin_hi


def _encode(xs, mems, mix_norm_g, w_in, hgrn_lb_logits, hgrn_gnorm_g, w_hgrn_o, w_attn_o, w_out,
            xa_norm_g, mem_norm_g, w_xq, w_xkv, w_xo, ffn_norm_g, w_ffn1, w_ffn2, final_norm_g,
            hgrn_rows=2048):
    s, d = xs[0].shape[1:]
    bsz = sum(x.shape[0] for x in xs)
    t = bsz * s
    depth = w_in.shape[0]
    assert depth == 1, "the final norm is fused into the (single) layer's MLP call"
    l = 0
    fd = HG_HEADS * HEAD
    lb_all = jnp.cumsum(jax.nn.softmax(hgrn_lb_logits.astype(F32), axis=1), axis=1)
    sizes = (fd,) * 5 + (ATTN_WIDTH,) * 9 + (d, d)
    offs = np.concatenate([[0], np.cumsum(sizes)])
    cos_t, sin_lo, sin_hi = _rotary_tables(s)
    tm = min(TOKEN_TILE, s)
    tmp = min(PROJ_TOKEN_TILE, s)
    tabspec = pl.BlockSpec((tmp, HEAD), lambda i: (i % (s // tmp), 0))
    srcs = [x.reshape(-1, d) for x in xs]
    mem = jnp.concatenate(mems, axis=0) if len(mems) > 1 else mems[0]

    wl = w_in[l]
    seg = lambda p: wl[:, offs[p]:offs[p + 1]]
    bf = lambda a: a.astype(BF16)
    w_silu = bf(jnp.concatenate([seg(0), seg(4)], axis=1))
    w_lf = bf(jnp.concatenate([seg(1), seg(2)], axis=1))
    w_plain = bf(jnp.concatenate([seg(3), seg(7), seg(10), seg(13)], axis=1))
    w_rot = bf(jnp.concatenate([seg(5), seg(6), seg(8), seg(9), seg(11), seg(12)], axis=1))
    w_gate = bf(jnp.concatenate([seg(14), seg(15)], axis=1))
    g_mix = mix_norm_g[l]

    a_silu, u = _first_proj(srcs, g_mix, w_silu, _ep_silu, BF16, tmp, "proj_silu")
    proj = functools.partial(_norm_proj, u, g_mix, tm=tmp, norm=False)
    lb_row = lb_all[:, l].reshape(1, 2 * fd)
    a_lf = proj(w_lf, _ep_logf, F32, extras=(lb_row,),
                extra_specs=(pl.BlockSpec((1, 2 * fd), lambda i: (0, 0)),), name="proj_logf")
    a_plain = proj(w_plain, _ep_plain, BF16, name="proj_plain")
    qscale = jnp.tile(jnp.concatenate([jnp.full((ATTN_WIDTH,), HEAD ** -0.5, F32),
                                       jnp.ones((ATTN_WIDTH,), F32)]), 3).reshape(1, 6 * ATTN_WIDTH)
    a_rot = proj(w_rot, _ep_rotary, BF16, extras=(cos_t, sin_lo, sin_hi, qscale),
                 extra_specs=(tabspec, tabspec, tabspec, pl.BlockSpec((1, 6 * ATTN_WIDTH), lambda i: (0, 0))),
                 name="proj_rotary")
    a_gate = proj(w_gate, _ep_sigmoid, BF16, name="proj_gate")

    a_silu3 = a_silu.reshape(bsz, s, 2 * fd)
    a_lf3 = a_lf.reshape(bsz, s, 2 * fd)
    a_plain3 = a_plain.reshape(bsz, s, -1)
    o_fwd = _hgrn_pass(a_silu3, a_lf3, a_plain3, hgrn_gnorm_g[l], None, False, hgrn_rows)
    hg = _hgrn_pass(a_silu3, a_lf3, a_plain3, hgrn_gnorm_g[l], o_fwd, True, hgrn_rows)

    attn = _dilated_attention(a_rot.reshape(bsz, s, -1), a_plain3)

    nm = mem.shape[1]
    kv = _norm_proj(mem.reshape(bsz * nm, d), mem_norm_g[l], w_xkv[l].astype(BF16), _ep_plain, BF16,
                    tm=nm, name="proj_memkv")
    h2d = _merge_xattn(srcs, hg.reshape(t, fd), attn.reshape(t, ATTN_WIDTH), a_gate,
                       w_hgrn_o[l].astype(BF16), w_attn_o[l].astype(BF16), w_out[l].astype(BF16),
                       xa_norm_g[l], w_xq[l].astype(BF16), kv.reshape(bsz, nm, 2 * d), w_xo[l].astype(BF16),
                       s, tm=tm)
    outs = _mlp(h2d, ffn_norm_g[l], w_ffn1[l].astype(BF16), w_ffn2[l].astype(BF16), final_norm_g,
                [x.shape[0] * s for x in xs], tm=tm)
    return [o.reshape(x.shape) for o, x in zip(outs, xs)]


def kernel(x_prompt, x_sample, mem_prompt, mem_sample, mix_norm_g, w_in, hgrn_lb_logits, hgrn_gnorm_g,
           w_hgrn_o, w_attn_o, w_out, xa_norm_g, mem_norm_g, w_xq, w_xkv, w_xo, ffn_norm_g, w_ffn1,
           w_ffn2, final_norm_g):
    assert x_prompt.shape[1:] == x_sample.shape[1:]
    y_prompt, y_sample = _encode(
        [x_prompt, x_sample], [mem_prompt, mem_sample], mix_norm_g, w_in, hgrn_lb_logits, hgrn_gnorm_g,
        w_hgrn_o, w_attn_o, w_out, xa_norm_g, mem_norm_g, w_xq, w_xkv, w_xo, ffn_norm_g, w_ffn1, w_ffn2,
        final_norm_g)
    return y_prompt, y_sample
```

```python
import functools

import numpy as np
import jax
import jax.numpy as jnp
from jax import lax
from jax.experimental import pallas as pl
from jax.experimental.pallas import tpu as pltpu

F32 = jnp.float32
BF16 = jnp.bfloat16

RMS_EPS = 1e-6
ROPE_THETA = 500000.0
HG_HEADS = 8
HEAD = 128
ATTN_HEADS = 4
ATTN_WIDTH = ATTN_HEADS * HEAD
ATTN_GROUPS = ((128, 1), (512, 4), (2048, 16))
ROT_DIM = HEAD // 4
XA_HEADS = 4
CHUNK = 64
SUB = 8
NEG = -1e30
HG_PACK = 2
SAFE_CHUNK_LOG_DECAY = -60.0
HGRN_UNROLL = 16
ATTN_UNROLL = 4
NT_DIMS = (((1,), (1,)), ((), ()))
TN_DIMS = (((0,), (0,)), ((), ()))

TOKEN_TILE = 512
PROJ_TOKEN_TILE = 1024

VMEM_LIMIT = 56 * 1024 * 1024


def _cparams(sem):
    return pltpu.CompilerParams(dimension_semantics=sem, vmem_limit_bytes=VMEM_LIMIT)


def _sigmoid(x):
    return 1.0 / (1.0 + jnp.exp(-x))


def _rms_scale(xf):
    return lax.rsqrt(jnp.mean(xf * xf, axis=-1, keepdims=True) + RMS_EPS)


def _ep_plain(acc):
    return acc


def _sigmoid_tanh(x):
    return 0.5 * jnp.tanh(0.5 * x) + 0.5


def _ep_silu(acc):
    return acc * _sigmoid_tanh(acc)


def _ep_sigmoid(acc):
    return _sigmoid_tanh(acc)


def _ep_logf(acc, lb):
    return jnp.log(lb + (1.0 - lb) * _sigmoid(acc))


def _ep_rotary(acc, cos, sin_lo, sin_hi, colscale):
    half = ROT_DIM // 2
    outs = []
    for c in range(acc.shape[1] // HEAD):
        t = acc[:, c * HEAD:(c + 1) * HEAD]
        r = t * cos + pltpu.roll(t, HEAD - half, 1) * sin_lo + pltpu.roll(t, half, 1) * sin_hi
        outs.append(r)
    return jnp.concatenate(outs, axis=1) * colscale


def _row_sources(srcs, tm):
    bounds = [0]
    for a in srcs:
        assert a.shape[0] % tm == 0
        bounds.append(bounds[-1] + a.shape[0] // tm)
    specs = [pl.BlockSpec((tm, a.shape[1]),
                          lambda i, lo=bounds[k], n=bounds[k + 1] - bounds[k]: (jnp.clip(i - lo, 0, n - 1), 0))
             for k, a in enumerate(srcs)]
    return specs, bounds


def _pick_source(i, refs, bounds):
    x = refs[-1][...]
    for k in range(len(refs) - 2, -1, -1):
        x = jnp.where(i < bounds[k + 1], refs[k][...], x)
    return x


def _first_proj_kernel(*refs, epilogue, bounds):
    nsrc = len(bounds) - 1
    g_ref, w_ref, o_ref, u_ref = refs[nsrc:]
    xf = _pick_source(pl.program_id(0), refs[:nsrc], bounds)
    u = (xf * _rms_scale(xf) * g_ref[...]).astype(BF16)
    u_ref[...] = u
    o_ref[...] = epilogue(jnp.dot(u, w_ref[...], preferred_element_type=F32)).astype(o_ref.dtype)


def _first_proj(srcs, g, w, epilogue, out_dtype, tm, name):
    d = srcs[0].shape[1]
    n = w.shape[1]
    specs, bounds = _row_sources(srcs, tm)
    m = bounds[-1] * tm
    return pl.pallas_call(
        functools.partial(_first_proj_kernel, epilogue=epilogue, bounds=bounds),
        grid=(bounds[-1],),
        in_specs=specs + [pl.BlockSpec((1, d), lambda i: (0, 0)), pl.BlockSpec((d, n), lambda i: (0, 0))],
        out_specs=[pl.BlockSpec((tm, n), lambda i: (i, 0)), pl.BlockSpec((tm, d), lambda i: (i, 0))],
        out_shape=[jax.ShapeDtypeStruct((m, n), out_dtype), jax.ShapeDtypeStruct((m, d), BF16)],
        compiler_params=_cparams(("parallel",)),
        name=name,
    )(*srcs, g.reshape(1, d), w)


def _norm_proj_kernel(*refs, epilogue, n_extra, norm):
    x_ref, g_ref, w_ref = refs[:3]
    extra = refs[3:3 + n_extra]
    o_ref = refs[3 + n_extra]
    if norm:
        xf = x_ref[...]
        u = (xf * _rms_scale(xf) * g_ref[...]).astype(BF16)
    else:
        u = x_ref[...]
    acc = jnp.dot(u, w_ref[...], preferred_element_type=F32)
    o_ref[...] = epilogue(acc, *[e[...] for e in extra]).astype(o_ref.dtype)


def _norm_proj(x2d, g, w, epilogue, out_dtype, extras=(), extra_specs=(), tm=512, name="norm_proj", norm=True):
    m, d = x2d.shape
    n = w.shape[1]
    tm = min(tm, m)
    assert m % tm == 0
    in_specs = [
        pl.BlockSpec((tm, d), lambda i: (i, 0)),
        pl.BlockSpec((1, d), lambda i: (0, 0)),
        pl.BlockSpec((d, n), lambda i: (0, 0)),
    ] + list(extra_specs)
    return pl.pallas_call(
        functools.partial(_norm_proj_kernel, epilogue=epilogue, n_extra=len(extras), norm=norm),
        grid=(m // tm,),
        in_specs=in_specs,
        out_specs=pl.BlockSpec((tm, n), lambda i: (i, 0)),
        out_shape=jax.ShapeDtypeStruct((m, n), out_dtype),
        compiler_params=_cparams(("parallel",)),
        name=name,
    )(x2d, g.reshape(1, d), w, *extras)


def _hgrn_robust_chunk(r0, ls, q_ref, lf_ref, v_ref, tri_ref, st_ref, kc_ref, bc_ref, rev):
    C = CHUNK
    ng = C // SUB
    rowl = lax.broadcasted_iota(jnp.int32, (C, HEAD), 0)
    sub_pos = rowl & (SUB - 1)
    arow = lax.broadcasted_iota(jnp.int32, (C, C), 0)
    acol = lax.broadcasted_iota(jnp.int32, (C, C), 1)
    ones_b = jnp.ones((HEAD, HEAD), BF16)
    zeros_g = jnp.zeros((SUB, HEAD), F32)

    lf = lf_ref[0, pl.ds(r0, C), ls]
    q = q_ref[0, pl.ds(r0, C), ls].astype(F32)
    v = v_ref[0, pl.ds(r0, C), ls]
    k = 1.0 - jnp.exp(lf)
    hi = lf.astype(BF16)
    mid = (lf - hi.astype(F32)).astype(BF16)
    b = jnp.dot(tri_ref[...], jnp.concatenate([hi, mid], axis=0), preferred_element_type=F32)
    kc_ref[pl.ds(SUB, C), :] = k
    bc_ref[pl.ds(SUB, C), :] = b
    b_end = bc_ref[pl.ds(SUB + (0 if rev else C - 1), 1), :]

    st = st_ref[:, ls]
    qi = (q * jnp.exp(b)).astype(BF16)
    o = lax.dot_general(qi, st.astype(BF16), NT_DIMS, preferred_element_type=F32)
    kl = (k * jnp.exp(b_end - b)).astype(BF16)
    st_ref[:, ls] = st * jnp.exp(b_end) + lax.dot_general(v, kl, TN_DIMS, preferred_element_type=F32)

    a = jnp.zeros((C, C), F32)
    h = C // 2
    while h >= SUB:
        qparts, kparts = [], []
        for gi in range(ng):
            t0 = gi * SUB
            blk = t0 // (2 * h)
            in_upper = (t0 % (2 * h)) >= h
            is_query = (not in_upper) if rev else in_upper
            rr = blk * 2 * h + (h if rev else h - 1)
            bref = bc_ref[pl.ds(SUB + rr, 1), :]
            bg = b[t0:t0 + SUB]
            if is_query:
                qparts.append(q[t0:t0 + SUB] * jnp.exp(bg - bref))
                kparts.append(zeros_g)
            else:
                qparts.append(zeros_g)
                kparts.append(k[t0:t0 + SUB] * jnp.exp(bref - bg))
        qh = jnp.concatenate(qparts, axis=0).astype(BF16)
        kh = jnp.concatenate(kparts, axis=0).astype(BF16)
        ah = lax.dot_general(qh, kh, NT_DIMS, preferred_element_type=F32)
        if 2 * h < C:
            ah = jnp.where((arow ^ acol) < 2 * h, ah, 0.0)
        a = a + ah
        h //= 2

    for d in range(SUB):
        sh = SUB + (d if rev else -d)
        ks = kc_ref[pl.ds(sh, C), :]
        bs = bc_ref[pl.ds(sh, C), :]
        ok = (sub_pos + d <= SUB - 1) if rev else (sub_pos >= d)
        p = jnp.where(ok, q * ks * jnp.exp(b - bs), 0.0).astype(BF16)
        rs = jnp.dot(p, ones_b, preferred_element_type=F32)[:, :C]
        tgt = (arow + d) if rev else (arow - d)
        a = a + jnp.where(acol == tgt, rs, 0.0)

    return o + jnp.dot(a.astype(BF16), v, preferred_element_type=F32)


def _blockdiag2(x):
    z = jnp.zeros((x.shape[0], HEAD), x.dtype)
    return jnp.concatenate([jnp.concatenate([x[:, :HEAD], z], axis=1),
                            jnp.concatenate([z, x[:, HEAD:]], axis=1)], axis=0)


def _hgrn_kernel(*refs, rev, final, nchunk):
    if final:
        (q_ref, lf_ref, v_ref, tri_ref, sel_ref, of_ref, g_ref, gn_ref, o_ref,
         st_ref, qi_ref, a_ref, upd_ref, snap_ref, dec_ref, b_ref, kc_ref, bc_ref) = refs
    else:
        (q_ref, lf_ref, v_ref, tri_ref, sel_ref, o_ref,
         st_ref, qi_ref, a_ref, upd_ref, snap_ref, dec_ref, b_ref, kc_ref, bc_ref) = refs
        of_ref = g_ref = gn_ref = None
    C = CHUNK
    W = HG_PACK * HEAD

    @pl.when(pl.program_id(2) == 0)
    def _():
        st_ref[...] = jnp.zeros_like(st_ref)
        kc_ref[...] = jnp.zeros_like(kc_ref)
        bc_ref[...] = jnp.zeros_like(bc_ref)

    def emit(o, r0, ls):
        if final:
            tot = o + of_ref[0, pl.ds(r0, C), ls]
            parts = []
            for j in range(tot.shape[1] // HEAD):
                tj = tot[:, j * HEAD:(j + 1) * HEAD]
                parts.append(tj * _rms_scale(tj) * gn_ref[...])
            y = parts[0] if len(parts) == 1 else jnp.concatenate(parts, axis=1)
            y = y * g_ref[0, pl.ds(r0, C), ls].astype(F32)
            o_ref[0, pl.ds(r0, C), ls] = y.astype(o_ref.dtype)
        else:
            o_ref[0, pl.ds(r0, C), ls] = o

    sums = jnp.dot(sel_ref[...], lf_ref[0].astype(BF16), preferred_element_type=F32)
    safe = jnp.min(sums) >= SAFE_CHUNK_LOG_DECAY

    @pl.when(safe)
    def _():
        arow = lax.broadcasted_iota(jnp.int32, (C, 2 * C), 0)
        acol = lax.broadcasted_iota(jnp.int32, (C, 2 * C), 1) & (C - 1)
        causal = (acol >= arow) if rev else (acol <= arow)

        G = HGRN_UNROLL
        assert nchunk % G == 0

        G2 = 2 * G
        assert nchunk % G2 == 0

        def phase0(i, carry):
            r0s = [pl.multiple_of((i * G2 + g) * C, C) for g in range(G2)]
            lfs = [lf_ref[0, pl.ds(r0, C), :] for r0 in r0s]
            his = [lf.astype(BF16) for lf in lfs]
            mids = [(lf - hi.astype(F32)).astype(BF16) for lf, hi in zip(lfs, his)]
            rhs = jnp.concatenate([jnp.concatenate(his, axis=1), jnp.concatenate(mids, axis=1)], axis=0)
            ball = jnp.dot(tri_ref[...], rhs, preferred_element_type=F32)
            for g in range(G2):
                b_ref[pl.ds(r0s[g], C), :] = ball[:, g * W:(g + 1) * W]
            return carry

        lax.fori_loop(0, nchunk // G2, phase0, 0)

        def phase1(i, carry):
            cs = [i * G + g for g in range(G)]
            r0s = [pl.multiple_of(c * C, C) for c in cs]
            lfs = [lf_ref[0, pl.ds(r0, C), :] for r0 in r0s]
            qis, kls, kbars = [], [], []
            for g in range(G):
                b = b_ref[pl.ds(r0s[g], C), :]
                k = 1.0 - jnp.exp(lfs[g])
                q = q_ref[0, pl.ds(r0s[g], C), :].astype(F32)
                qi = (q * jnp.exp(b)).astype(BF16)
                qi_ref[pl.ds(r0s[g], C), :] = qi
                dec = jnp.exp(b[0:1] if rev else b[C - 1:C])
                dec_ref[pl.ds(pl.multiple_of(cs[g] * SUB, SUB), SUB), :] = jnp.broadcast_to(dec, (SUB, W))
                qis.append(qi)
                kbar = k * jnp.exp(-b)
                kls.append((kbar * dec).astype(BF16))
                kbars.append(kbar.astype(BF16))
            avals = [lax.dot_general(qis[g], _blockdiag2(kbars[g]), NT_DIMS, preferred_element_type=F32)
                     for g in range(G)]
            for g in range(G):
                v = v_ref[0, pl.ds(r0s[g], C), :]
                for j in range(HG_PACK):
                    ls = slice(j * HEAD, (j + 1) * HEAD)
                    upd_ref[pl.ds(pl.multiple_of(cs[g] * HEAD, HEAD), HEAD), ls] = lax.dot_general(
                        v[:, ls], kls[g][:, ls], TN_DIMS, preferred_element_type=F32)
            for g in range(G):
                a_ref[pl.ds(r0s[g], C), :] = jnp.where(causal, avals[g], 0.0).astype(BF16)
            return carry

        lax.fori_loop(0, nchunk // G, phase1, 0)

        def phase2(i, carry):
            c = (nchunk - 1 - i) if rev else i
            s0 = pl.multiple_of(c * HEAD, HEAD)
            st = st_ref[...]
            snap_ref[pl.ds(s0, HEAD), :] = st.astype(BF16)
            dec = dec_ref[pl.ds(pl.multiple_of(c * SUB, SUB), 1), :]
            st_ref[...] = st * dec + upd_ref[pl.ds(s0, HEAD), :]
            return carry

        lax.fori_loop(0, nchunk, phase2, 0, unroll=G)

        def phase3(i, carry):
            cs = [i * G + g for g in range(G)]
            r0s = [pl.multiple_of(c * C, C) for c in cs]
            snaps = [snap_ref[pl.ds(pl.multiple_of(c * HEAD, HEAD), HEAD), :] for c in cs]
            o1 = [lax.dot_general(qi_ref[pl.ds(r0s[g], C), :], _blockdiag2(snaps[g]), NT_DIMS,
                                  preferred_element_type=F32) for g in range(G)]
            o2 = [jnp.dot(a_ref[pl.ds(r0s[g], C), :], _blockdiag2(v_ref[0, pl.ds(r0s[g], C), :]),
                          preferred_element_type=F32) for g in range(G)]
            for g in range(G):
                emit(o1[g] + o2[g], r0s[g], slice(None))
            return carry

        lax.fori_loop(0, nchunk // G, phase3, 0)

    @pl.when(jnp.logical_not(safe))
    def _():
        def body(i, carry):
            c = (nchunk - 1 - i) if rev else i
            r0 = pl.multiple_of(c * C, C)
            for j in range(HG_PACK):
                ls = slice(j * HEAD, (j + 1) * HEAD)
                o = _hgrn_robust_chunk(r0, ls, q_ref, lf_ref, v_ref, tri_ref, st_ref, kc_ref, bc_ref, rev)
                emit(o, r0, ls)
            return carry

        lax.fori_loop(0, nchunk, body, 0)


def _tri_matrix(rev):
    t = np.arange(CHUNK)
    m = (t[None, :] >= t[:, None]) if rev else (t[None, :] <= t[:, None])
    return jnp.asarray(np.concatenate([m, m], axis=1).astype(np.float32), dtype=BF16)


def _chunk_sum_matrix(rows):
    nchunk = rows // CHUNK
    nsel = -(-nchunk // 16) * 16
    m = np.zeros((nsel, rows), np.float32)
    m[np.arange(rows) // CHUNK, np.arange(rows)] = 1.0
    return jnp.asarray(m, dtype=BF16)


def _hgrn_pass(a_silu, a_lf, a_plain, gnorm, o_fwd, rev, rows):
    bsz, s, _ = a_silu.shape
    rows = min(rows, s)
    nblk = s // rows
    nchunk = rows // CHUNK
    w = HG_PACK * HEAD
    npk = HG_HEADS // HG_PACK
    final = o_fwd is not None
    seq = (lambda i: nblk - 1 - i) if rev else (lambda i: i)
    fcol = npk if rev else 0
    sel = _chunk_sum_matrix(rows)
    in_specs = [
        pl.BlockSpec((1, rows, w), lambda b, h, i: (b, seq(i), h)),
        pl.BlockSpec((1, rows, w), lambda b, h, i: (b, seq(i), h + fcol)),
        pl.BlockSpec((1, rows, w), lambda b, h, i: (b, seq(i), h)),
        pl.BlockSpec((CHUNK, 2 * CHUNK), lambda b, h, i: (0, 0)),
        pl.BlockSpec(sel.shape, lambda b, h, i: (0, 0)),
    ]
    args = [a_silu, a_lf, a_plain, _tri_matrix(rev), sel]
    if final:
        in_specs += [
            pl.BlockSpec((1, rows, w), lambda b, h, i: (b, seq(i), h)),
            pl.BlockSpec((1, rows, w), lambda b, h, i: (b, seq(i), h + npk)),
            pl.BlockSpec((1, HEAD), lambda b, h, i: (0, 0)),
        ]
        args += [o_fwd, a_silu, gnorm.reshape(1, HEAD)]
    return pl.pallas_call(
        functools.partial(_hgrn_kernel, rev=rev, final=final, nchunk=nchunk),
        grid=(bsz, npk, nblk),
        in_specs=in_specs,
        out_specs=pl.BlockSpec((1, rows, w), lambda b, h, i: (b, seq(i), h)),
        out_shape=jax.ShapeDtypeStruct((bsz, s, HG_HEADS * HEAD), BF16 if final else F32),
        scratch_shapes=[
            pltpu.VMEM((HEAD, w), F32),
            pltpu.VMEM((rows, w), BF16),
            pltpu.VMEM((rows, 2 * CHUNK), BF16),
            pltpu.VMEM((nchunk * HEAD, w), F32),
            pltpu.VMEM((nchunk * HEAD, w), BF16),
            pltpu.VMEM((nchunk * SUB, w), F32),
            pltpu.VMEM((rows, w), F32),
            pltpu.VMEM((CHUNK + 2 * SUB, HEAD), F32),
            pltpu.VMEM((CHUNK + 2 * SUB, HEAD), F32),
        ],
        compiler_params=_cparams(("parallel", "parallel", "arbitrary")),
        name="hgrn_bwd" if rev else "hgrn_fwd",
    )(*args)


def _attn_group(q_ref, k_ref, v_ref, qs_ref, ks_ref, vs_ref, acc_ref, m_ref, l_ref, *, dil, span, first):
    s = q_ref.shape[1]
    length = s // dil
    qt, kt = 2 * span, 4 * span
    ntile = length // qt
    dcol = (lax.broadcasted_iota(jnp.int32, (qt, kt), 1) - lax.broadcasted_iota(jnp.int32, (qt, kt), 0))
    if dil > 1:
        qs_ref[...] = q_ref[0].astype(F32)
        ks_ref[...] = k_ref[0].astype(F32)
        vs_ref[...] = v_ref[0].astype(F32)

    G = min(ATTN_UNROLL, ntile)
    assert ntile % G == 0
    for r in range(dil):
        def tiles(i, carry):
            js = [i * G + g for g in range(G)]
            w0s = [jnp.clip(j * qt - span, 0, length - kt) for j in js]
            qbases = [pl.multiple_of(j * (qt * dil), qt * dil) for j in js]
            kbases = [pl.multiple_of(w0 * dil, span * dil) for w0 in w0s]
            rows = pl.ds(r, qt, stride=dil) if dil > 1 else slice(None)

            def window(ref, sref, base, n):
                if dil > 1:
                    return sref.at[pl.ds(base, n * dil)][pl.ds(r, n, stride=dil), :].astype(BF16)
                return ref[0, pl.ds(base, n), :]

            scs = [lax.dot_general(window(q_ref, qs_ref, qbases[g], qt), window(k_ref, ks_ref, kbases[g], kt),
                                   NT_DIMS, preferred_element_type=F32) for g in range(G)]
            ps, mns, lns = [], [], []
            for g in range(G):
                sc = jnp.where(jnp.abs(dcol + (w0s[g] - js[g] * qt)) <= span, scs[g], NEG)
                mx = jnp.max(sc, axis=-1, keepdims=True)
                p = jnp.exp(sc - mx)
                lns.append(jnp.broadcast_to(jnp.sum(p, axis=-1, keepdims=True), (qt, HEAD)))
                mns.append(jnp.broadcast_to(mx, (qt, HEAD)))
                ps.append(p.astype(BF16))
            ons = [jnp.dot(ps[g], window(v_ref, vs_ref, kbases[g], kt), preferred_element_type=F32)
                   for g in range(G)]
            for g in range(G):
                mv = m_ref.at[pl.ds(qbases[g], qt * dil)]
                lv = l_ref.at[pl.ds(qbases[g], qt * dil)]
                av = acc_ref.at[pl.ds(qbases[g], qt * dil)]
                if first:
                    mv[rows, :] = mns[g]
                    lv[rows, :] = lns[g]
                    av[rows, :] = ons[g]
                else:
                    mo = mv[rows, :]
                    mm = jnp.maximum(mo, mns[g])
                    wo = jnp.exp(mo - mm)
                    wn = jnp.exp(mns[g] - mm)
                    mv[rows, :] = mm
                    lv[rows, :] = wo * lv[rows, :] + wn * lns[g]
                    av[rows, :] = wo * av[rows, :] + wn * ons[g]
            return carry

        lax.fori_loop(0, ntile // G, tiles, 0)


def _attn_kernel(q_ref, k_ref, v_ref, o_ref, qs_ref, ks_ref, vs_ref, acc_ref, m_ref, l_ref):
    g = pl.program_id(2)
    for gi, (win, dil) in enumerate(ATTN_GROUPS):
        @pl.when(g == gi)
        def _(gi=gi, win=win, dil=dil):
            _attn_group(q_ref, k_ref, v_ref, qs_ref, ks_ref, vs_ref, acc_ref, m_ref, l_ref,
                        dil=dil, span=(win // 2) // dil, first=(gi == 0))

    @pl.when(g == len(ATTN_GROUPS) - 1)
    def _():
        o_ref[0] = (acc_ref[...] * (1.0 / l_ref[...])).astype(o_ref.dtype)


def _dilated_attention(a_rot, a_plain):
    bsz, s, _ = a_rot.shape
    ng = len(ATTN_GROUPS)
    vbase = HG_HEADS
    blk = (1, s, HEAD)
    return pl.pallas_call(
        _attn_kernel,
        grid=(bsz, ATTN_HEADS, ng),
        in_specs=[pl.BlockSpec(blk, lambda b, h, g: (b, 0, 2 * g * ATTN_HEADS + h)),
                  pl.BlockSpec(blk, lambda b, h, g: (b, 0, (2 * g + 1) * ATTN_HEADS + h)),
                  pl.BlockSpec(blk, lambda b, h, g: (b, 0, vbase + g * ATTN_HEADS + h))],
        out_specs=pl.BlockSpec(blk, lambda b, h, g: (b, 0, h)),
        out_shape=jax.ShapeDtypeStruct((bsz, s, ATTN_WIDTH), BF16),
        scratch_shapes=[pltpu.VMEM((s, HEAD), F32)] * 6,
        compiler_params=_cparams(("parallel", "parallel", "arbitrary")),
        name="dilated_attn",
    )(a_rot, a_rot, a_plain)


def _merge_xattn_kernel(*refs, bounds):
    nsrc = len(bounds) - 1
    (hg_ref, attn_ref, gate_ref, who_ref, wao_ref, wout_ref,
     g_ref, wq_ref, kv_ref, wo_ref, out_ref) = refs[nsrc:]
    x = _pick_source(pl.program_id(0), refs[:nsrc], bounds)
    d = x.shape[1]
    hd = d // XA_HEADS
    yh = jnp.dot(hg_ref[...], who_ref[...], preferred_element_type=F32)
    ya = jnp.dot(attn_ref[...], wao_ref[...], preferred_element_type=F32)
    merged = gate_ref[:, :d].astype(F32) * yh + gate_ref[:, d:].astype(F32) * ya
    hx = x + jnp.dot(merged.astype(BF16), wout_ref[...], preferred_element_type=F32)
    u = (hx * _rms_scale(hx) * g_ref[...]).astype(BF16)
    q = (jnp.dot(u, wq_ref[...], preferred_element_type=F32) * (hd ** -0.5)).astype(BF16)
    outs = []
    for h in range(XA_HEADS):
        kh = kv_ref[0, :, h * hd:(h + 1) * hd]
        vh = kv_ref[0, :, d + h * hd:d + (h + 1) * hd]
        s = lax.dot_general(q[:, h * hd:(h + 1) * hd], kh, NT_DIMS, preferred_element_type=F32)
        p = jnp.exp(s - jnp.max(s, axis=-1, keepdims=True))
        den = jnp.sum(p, axis=-1, keepdims=True)
        outs.append(jnp.dot(p.astype(BF16), vh, preferred_element_type=F32) * (1.0 / den))
    o = jnp.concatenate(outs, axis=1).astype(BF16)
    out_ref[...] = hx + jnp.dot(o, wo_ref[...], preferred_element_type=F32)


def _merge_xattn(srcs, hg, attn, gates, who, wao, wout, g, wq, kv, wo, seq_len, tm=512):
    d = srcs[0].shape[1]
    xspecs, bounds = _row_sources(srcs, tm)
    tiles_per_seq = seq_len // tm
    row = lambda w: pl.BlockSpec((tm, w), lambda i: (i, 0))
    full = lambda a: pl.BlockSpec(a.shape, lambda i: (0, 0))
    return pl.pallas_call(
        functools.partial(_merge_xattn_kernel, bounds=bounds),
        grid=(bounds[-1],),
        in_specs=xspecs + [row(d), row(ATTN_WIDTH), row(2 * d), full(who), full(wao), full(wout),
                           pl.BlockSpec((1, d), lambda i: (0, 0)), full(wq),
                           pl.BlockSpec((1,) + kv.shape[1:], lambda i: (i // tiles_per_seq, 0, 0)), full(wo)],
        out_specs=row(d),
        out_shape=jax.ShapeDtypeStruct((bounds[-1] * tm, d), F32),
        compiler_params=_cparams(("parallel",)),
        name="merge_xattn",
    )(*srcs, hg, attn, gates, who, wao, wout, g.reshape(1, d), wq, kv, wo)


def _mlp_kernel(h_ref, g_ref, w1_ref, w2_ref, gf_ref, *out_refs, bounds):
    hx = h_ref[...]
    u = (hx * _rms_scale(hx) * g_ref[...]).astype(BF16)
    a = jnp.maximum(jnp.dot(u, w1_ref[...], preferred_element_type=F32), 0.0)
    a = (a * a).astype(BF16)
    y = hx + jnp.dot(a, w2_ref[...], preferred_element_type=F32)
    y = y * _rms_scale(y) * gf_ref[...]
    i = pl.program_id(0)
    for k, out_ref in enumerate(out_refs):
        @pl.when(jnp.logical_and(i >= bounds[k], i < bounds[k + 1]))
        def _(out_ref=out_ref):
            out_ref[...] = y


def _mlp(h2d, g, w1, w2, gf, group_rows, tm=512):
    m, d = h2d.shape
    assert sum(group_rows) == m
    bounds = [0]
    for r in group_rows:
        assert r % tm == 0
        bounds.append(bounds[-1] + r // tm)
    full = lambda a: pl.BlockSpec(a.shape, lambda i: (0, 0))
    out_specs = [pl.BlockSpec((tm, d), lambda i, lo=bounds[k], n=bounds[k + 1] - bounds[k]:
                              (jnp.clip(i - lo, 0, n - 1), 0)) for k in range(len(group_rows))]
    return pl.pallas_call(
        functools.partial(_mlp_kernel, bounds=bounds),
        grid=(m // tm,),
        in_specs=[pl.BlockSpec((tm, d), lambda i: (i, 0)), pl.BlockSpec((1, d), lambda i: (0, 0)),
                  full(w1), full(w2), pl.BlockSpec((1, d), lambda i: (0, 0))],
        out_specs=out_specs,
        out_shape=[jax.ShapeDtypeStruct((r, d), F32) for r in group_rows],
        compiler_params=_cparams(("arbitrary",)),
        name="mlp_final",
    )(h2d, g.reshape(1, d), w1, w2, gf.reshape(1, d))


def _rotary_tables(s):
    half = ROT_DIM // 2
    inv = ROPE_THETA ** (-jnp.arange(half, dtype=F32) * 2.0 / ROT_DIM)
    ang = jnp.arange(s, dtype=F32)[:, None] * inv[None, :]
    cos, sin = jnp.cos(ang), jnp.sin(ang)
    pad = jnp.zeros((s, HEAD - ROT_DIM), F32)
    zero = jnp.zeros((s, half), F32)
    cos_t = jnp.concatenate([cos, cos, pad + 1.0], axis=1)
    sin_lo = jnp.concatenate([-sin, zero, pad], axis=1)
    sin_hi = jnp.concatenate([zero, sin, pad], axis=1)
    return cos_t, sin_lo, sin_hi


def _encode(xs, mems, mix_norm_g, w_in, hgrn_lb_logits, hgrn_gnorm_g, w_hgrn_o, w_attn_o, w_out,
            xa_norm_g, mem_norm_g, w_xq, w_xkv, w_xo, ffn_norm_g, w_ffn1, w_ffn2, final_norm_g,
            hgrn_rows=2048):
    s, d = xs[0].shape[1:]
    bsz = sum(x.shape[0] for x in xs)
    t = bsz * s
    depth = w_in.shape[0]
    assert depth == 1, "the final norm is fused into the (single) layer's MLP call"
    l = 0
    fd = HG_HEADS * HEAD
    lb_all = jnp.cumsum(jax.nn.softmax(hgrn_lb_logits.astype(F32), axis=1), axis=1)
    sizes = (fd,) * 5 + (ATTN_WIDTH,) * 9 + (d, d)
    offs = np.concatenate([[0], np.cumsum(sizes)])
    cos_t, sin_lo, sin_hi = _rotary_tables(s)
    tm = min(TOKEN_TILE, s)
    tmp = min(PROJ_TOKEN_TILE, s)
    tabspec = pl.BlockSpec((tmp, HEAD), lambda i: (i % (s // tmp), 0))
    srcs = [x.reshape(-1, d) for x in xs]
    mem = jnp.concatenate(mems, axis=0) if len(mems) > 1 else mems[0]

    wl = w_in[l]
    seg = lambda p: wl[:, offs[p]:offs[p + 1]]
    bf = lambda a: a.astype(BF16)
    w_silu = bf(jnp.concatenate([seg(0), seg(4)], axis=1))
    w_lf = bf(jnp.concatenate([seg(1), seg(2)], axis=1))
    w_plain = bf(jnp.concatenate([seg(3), seg(7), seg(10), seg(13)], axis=1))
    w_rot = bf(jnp.concatenate([seg(5), seg(6), seg(8), seg(9), seg(11), seg(12)], axis=1))
    w_gate = bf(jnp.concatenate([seg(14), seg(15)], axis=1))
    g_mix = mix_norm_g[l]

    a_silu, u = _first_proj(srcs, g_mix, w_silu, _ep_silu, BF16, tmp, "proj_silu")
    proj = functools.partial(_norm_proj, u, g_mix, tm=tmp, norm=False)
    lb_row = lb_all[:, l].reshape(1, 2 * fd)
    a_lf = proj(w_lf, _ep_logf, F32, extras=(lb_row,),
                extra_specs=(pl.BlockSpec((1, 2 * fd), lambda i: (0, 0)),), name="proj_logf")
    a_plain = proj(w_plain, _ep_plain, BF16, name="proj_plain")
    qscale = jnp.tile(jnp.concatenate([jnp.full((ATTN_WIDTH,), HEAD ** -0.5, F32),
                                       jnp.ones((ATTN_WIDTH,), F32)]), 3).reshape(1, 6 * ATTN_WIDTH)
    a_rot = proj(w_rot, _ep_rotary, BF16, extras=(cos_t, sin_lo, sin_hi, qscale),
                 extra_specs=(tabspec, tabspec, tabspec, pl.BlockSpec((1, 6 * ATTN_WIDTH), lambda i: (0, 0))),
                 name="proj_rotary")
    a_gate = proj(w_gate, _ep_sigmoid, BF16, name="proj_gate")

    a_silu3 = a_silu.reshape(bsz, s, 2 * fd)
    a_lf3 = a_lf.reshape(bsz, s, 2 * fd)
    a_plain3 = a_plain.reshape(bsz, s, -1)
    o_fwd = _hgrn_pass(a_silu3, a_lf3, a_plain3, hgrn_gnorm_g[l], None, False, hgrn_rows)
    hg = _hgrn_pass(a_silu3, a_lf3, a_plain3, hgrn_gnorm_g[l], o_fwd, True, hgrn_rows)

    attn = _dilated_attention(a_rot.reshape(bsz, s, -1), a_plain3)

    nm = mem.shape[1]
    kv = _norm_proj(mem.reshape(bsz * nm, d), mem_norm_g[l], w_xkv[l].astype(BF16), _ep_plain, BF16,
                    tm=nm, name="proj_memkv")
    h2d = _merge_xattn(srcs, hg.reshape(t, fd), attn.reshape(t, ATTN_WIDTH), a_gate,
                       w_hgrn_o[l].astype(BF16), w_attn_o[l].astype(BF16), w_out[l].astype(BF16),
                       xa_norm_g[l], w_xq[l].astype(BF16), kv.reshape(bsz, nm, 2 * d), w_xo[l].astype(BF16),
                       s, tm=tm)
    outs = _mlp(h2d, ffn_norm_g[l], w_ffn1[l].astype(BF16), w_ffn2[l].astype(BF16), final_norm_g,
                [x.shape[0] * s for x in xs], tm=tm)
    return [o.reshape(x.shape) for o, x in zip(outs, xs)]


def kernel(x_prompt, x_sample, mem_prompt, mem_sample, mix_norm_g, w_in, hgrn_lb_logits, hgrn_gnorm_g,
           w_hgrn_o, w_attn_o, w_out, xa_norm_g, mem_norm_g, w_xq, w_xkv, w_xo, ffn_norm_g, w_ffn1,
           w_ffn2, final_norm_g):
    assert x_prompt.shape[1:] == x_sample.shape[1:]
    y_prompt, y_sample = _encode(
        [x_prompt, x_sample], [mem_prompt, mem_sample], mix_norm_g, w_in, hgrn_lb_logits, hgrn_gnorm_g,
        w_hgrn_o, w_attn_o, w_out, xa_norm_g, mem_norm_g, w_xq, w_xkv, w_xo, ffn_norm_g, w_ffn1, w_ffn2,
        final_norm_g)
    return y_prompt, y_sample
```

```python
import functools

import numpy as np
import jax
import jax.numpy as jnp
from jax import lax
from jax.experimental import pallas as pl
from jax.experimental.pallas import tpu as pltpu

F32 = jnp.float32
BF16 = jnp.bfloat16

RMS_EPS = 1e-6
ROPE_THETA = 500000.0
HG_HEADS = 8
HEAD = 128
ATTN_HEADS = 4
ATTN_WIDTH = ATTN_HEADS * HEAD
ATTN_GROUPS = ((128, 1), (512, 4), (2048, 16))
ROT_DIM = HEAD // 4
XA_HEADS = 4
CHUNK = 64
SUB = 8
NEG = -1e30
HG_PACK = 2
SAFE_CHUNK_LOG_DECAY = -60.0
HGRN_UNROLL = 16
ATTN_UNROLL = 4
NT_DIMS = (((1,), (1,)), ((), ()))
TN_DIMS = (((0,), (0,)), ((), ()))

TOKEN_TILE = 512
PROJ_TOKEN_TILE = 1024

VMEM_LIMIT = 56 * 1024 * 1024


def _cparams(sem):
    return pltpu.CompilerParams(dimension_semantics=sem, vmem_limit_bytes=VMEM_LIMIT)


def _sigmoid(x):
    return 1.0 / (1.0 + jnp.exp(-x))


def _rms_scale(xf):
    return lax.rsqrt(jnp.mean(xf * xf, axis=-1, keepdims=True) + RMS_EPS)


def _ep_plain(acc):
    return acc


def _sigmoid_tanh(x):
    return 0.5 * jnp.tanh(0.5 * x) + 0.5


def _ep_silu(acc):
    return acc * _sigmoid_tanh(acc)


def _ep_sigmoid(acc):
    return _sigmoid_tanh(acc)


def _ep_logf(acc, lb):
    return jnp.log(lb + (1.0 - lb) * _sigmoid(acc))


def _ep_rotary(acc, cos, sin_lo, sin_hi, colscale):
    half = ROT_DIM // 2
    outs = []
    for c in range(acc.shape[1] // HEAD):
        t = acc[:, c * HEAD:(c + 1) * HEAD]
        r = t * cos + pltpu.roll(t, HEAD - half, 1) * sin_lo + pltpu.roll(t, half, 1) * sin_hi
        outs.append(r)
    return jnp.concatenate(outs, axis=1) * colscale


def _row_sources(srcs, tm):
    bounds = [0]
    for a in srcs:
        assert a.shape[0] % tm == 0
        bounds.append(bounds[-1] + a.shape[0] // tm)
    specs = [pl.BlockSpec((tm, a.shape[1]),
                          lambda i, lo=bounds[k], n=bounds[k + 1] - bounds[k]: (jnp.clip(i - lo, 0, n - 1), 0))
             for k, a in enumerate(srcs)]
    return specs, bounds


def _pick_source(i, refs, bounds):
    x = refs[-1][...]
    for k in range(len(refs) - 2, -1, -1):
        x = jnp.where(i < bounds[k + 1], refs[k][...], x)
    return x


def _first_proj_kernel(*refs, epilogue, bounds):
    nsrc = len(bounds) - 1
    g_ref, w_ref, o_ref, u_ref = refs[nsrc:]
    xf = _pick_source(pl.program_id(0), refs[:nsrc], bounds)
    u = (xf * _rms_scale(xf) * g_ref[...]).astype(BF16)
    u_ref[...] = u
    o_ref[...] = epilogue(jnp.dot(u, w_ref[...], preferred_element_type=F32)).astype(o_ref.dtype)


def _first_proj(srcs, g, w, epilogue, out_dtype, tm, name):
    d = srcs[0].shape[1]
    n = w.shape[1]
    specs, bounds = _row_sources(srcs, tm)
    m = bounds[-1] * tm
    return pl.pallas_call(
        functools.partial(_first_proj_kernel, epilogue=epilogue, bounds=bounds),
        grid=(bounds[-1],),
        in_specs=specs + [pl.BlockSpec((1, d), lambda i: (0, 0)), pl.BlockSpec((d, n), lambda i: (0, 0))],
        out_specs=[pl.BlockSpec((tm, n), lambda i: (i, 0)), pl.BlockSpec((tm, d), lambda i: (i, 0))],
        out_shape=[jax.ShapeDtypeStruct((m, n), out_dtype), jax.ShapeDtypeStruct((m, d), BF16)],
        compiler_params=_cparams(("parallel",)),
        name=name,
    )(*srcs, g.reshape(1, d), w)


def _norm_proj_kernel(*refs, epilogue, n_extra, norm):
    x_ref, g_ref, w_ref = refs[:3]
    extra = refs[3:3 + n_extra]
    o_ref = refs[3 + n_extra]
    if norm:
        xf = x_ref[...]
        u = (xf * _rms_scale(xf) * g_ref[...]).astype(BF16)
    else:
        u = x_ref[...]
    acc = jnp.dot(u, w_ref[...], preferred_element_type=F32)
    o_ref[...] = epilogue(acc, *[e[...] for e in extra]).astype(o_ref.dtype)


def _norm_proj(x2d, g, w, epilogue, out_dtype, extras=(), extra_specs=(), tm=512, name="norm_proj", norm=True):
    m, d = x2d.shape
    n = w.shape[1]
    tm = min(tm, m)
    assert m % tm == 0
    in_specs = [
        pl.BlockSpec((tm, d), lambda i: (i, 0)),
        pl.BlockSpec((1, d), lambda i: (0, 0)),
        pl.BlockSpec((d, n), lambda i: (0, 0)),
    ] + list(extra_specs)
    return pl.pallas_call(
        functools.partial(_norm_proj_kernel, epilogue=epilogue, n_extra=len(extras), norm=norm),
        grid=(m // tm,),
        in_specs=in_specs,
        out_specs=pl.BlockSpec((tm, n), lambda i: (i, 0)),
        out_shape=jax.ShapeDtypeStruct((m, n), out_dtype),
        compiler_params=_cparams(("parallel",)),
        name=name,
    )(x2d, g.reshape(1, d), w, *extras)


def _hgrn_robust_chunk(r0, ls, q_ref, lf_ref, v_ref, tri_ref, st_ref, kc_ref, bc_ref, rev):
    C = CHUNK
    ng = C // SUB
    rowl = lax.broadcasted_iota(jnp.int32, (C, HEAD), 0)
    sub_pos = rowl & (SUB - 1)
    arow = lax.broadcasted_iota(jnp.int32, (C, C), 0)
    acol = lax.broadcasted_iota(jnp.int32, (C, C), 1)
    ones_b = jnp.ones((HEAD, HEAD), BF16)
    zeros_g = jnp.zeros((SUB, HEAD), F32)

    lf = lf_ref[0, pl.ds(r0, C), ls]
    q = q_ref[0, pl.ds(r0, C), ls].astype(F32)
    v = v_ref[0, pl.ds(r0, C), ls]
    k = 1.0 - jnp.exp(lf)
    hi = lf.astype(BF16)
    mid = (lf - hi.astype(F32)).astype(BF16)
    b = jnp.dot(tri_ref[...], jnp.concatenate([hi, mid], axis=0), preferred_element_type=F32)
    kc_ref[pl.ds(SUB, C), :] = k
    bc_ref[pl.ds(SUB, C), :] = b
    b_end = bc_ref[pl.ds(SUB + (0 if rev else C - 1), 1), :]

    st = st_ref[:, ls]
    qi = (q * jnp.exp(b)).astype(BF16)
    o = lax.dot_general(qi, st.astype(BF16), NT_DIMS, preferred_element_type=F32)
    kl = (k * jnp.exp(b_end - b)).astype(BF16)
    st_ref[:, ls] = st * jnp.exp(b_end) + lax.dot_general(v, kl, TN_DIMS, preferred_element_type=F32)

    a = jnp.zeros((C, C), F32)
    h = C // 2
    while h >= SUB:
        qparts, kparts = [], []
        for gi in range(ng):
            t0 = gi * SUB
            blk = t0 // (2 * h)
            in_upper = (t0 % (2 * h)) >= h
            is_query = (not in_upper) if rev else in_upper
            rr = blk * 2 * h + (h if rev else h - 1)
            bref = bc_ref[pl.ds(SUB + rr, 1), :]
            bg = b[t0:t0 + SUB]
            if is_query:
                qparts.append(q[t0:t0 + SUB] * jnp.exp(bg - bref))
                kparts.append(zeros_g)
            else:
                qparts.append(zeros_g)
                kparts.append(k[t0:t0 + SUB] * jnp.exp(bref - bg))
        qh = jnp.concatenate(qparts, axis=0).astype(BF16)
        kh = jnp.concatenate(kparts, axis=0).astype(BF16)
        ah = lax.dot_general(qh, kh, NT_DIMS, preferred_element_type=F32)
        if 2 * h < C:
            ah = jnp.where((arow ^ acol) < 2 * h, ah, 0.0)
        a = a + ah
        h //= 2

    for d in range(SUB):
        sh = SUB + (d if rev else -d)
        ks = kc_ref[pl.ds(sh, C), :]
        bs = bc_ref[pl.ds(sh, C), :]
        ok = (sub_pos + d <= SUB - 1) if rev else (sub_pos >= d)
        p = jnp.where(ok, q * ks * jnp.exp(b - bs), 0.0).astype(BF16)
        rs = jnp.dot(p, ones_b, preferred_element_type=F32)[:, :C]
        tgt = (arow + d) if rev else (arow - d)
        a = a + jnp.where(acol == tgt, rs, 0.0)

    return o + jnp.dot(a.astype(BF16), v, preferred_element_type=F32)


def _blockdiag2(x):
    z = jnp.zeros((x.shape[0], HEAD), x.dtype)
    return jnp.concatenate([jnp.concatenate([x[:, :HEAD], z], axis=1),
                            jnp.concatenate([z, x[:, HEAD:]], axis=1)], axis=0)


def _hgrn_kernel(*refs, rev, final, nchunk):
    if final:
        (q_ref, lf_ref, v_ref, tri_ref, sel_ref, of_ref, g_ref, gn_ref, o_ref,
         st_ref, qi_ref, a_ref, upd_ref, snap_ref, dec_ref, b_ref, kc_ref, bc_ref) = refs
    else:
        (q_ref, lf_ref, v_ref, tri_ref, sel_ref, o_ref,
         st_ref, qi_ref, a_ref, upd_ref, snap_ref, dec_ref, b_ref, kc_ref, bc_ref) = refs
        of_ref = g_ref = gn_ref = None
    C = CHUNK
    W = HG_PACK * HEAD

    @pl.when(pl.program_id(2) == 0)
    def _():
        st_ref[...] = jnp.zeros_like(st_ref)
        kc_ref[...] = jnp.zeros_like(kc_ref)
        bc_ref[...] = jnp.zeros_like(bc_ref)

    def emit(o, r0, ls):
        if final:
            tot = o + of_ref[0, pl.ds(r0, C), ls]
            parts = []
            for j in range(tot.shape[1] // HEAD):
                tj = tot[:, j * HEAD:(j + 1) * HEAD]
                parts.append(tj * _rms_scale(tj) * gn_ref[...])
            y = parts[0] if len(parts) == 1 else jnp.concatenate(parts, axis=1)
            y = y * g_ref[0, pl.ds(r0, C), ls].astype(F32)
            o_ref[0, pl.ds(r0, C), ls] = y.astype(o_ref.dtype)
        else:
            o_ref[0, pl.ds(r0, C), ls] = o

    sums = jnp.dot(sel_ref[...], lf_ref[0].astype(BF16), preferred_element_type=F32)
    safe = jnp.min(sums) >= SAFE_CHUNK_LOG_DECAY

    @pl.when(safe)
    def _():
        arow = lax.broadcasted_iota(jnp.int32, (C, 2 * C), 0)
        acol = lax.broadcasted_iota(jnp.int32, (C, 2 * C), 1) & (C - 1)
        causal = (acol >= arow) if rev else (acol <= arow)

        G = HGRN_UNROLL
        assert nchunk % G == 0

        G2 = 2 * G
        assert nchunk % G2 == 0

        def phase0(i, carry):
            r0s = [pl.multiple_of((i * G2 + g) * C, C) for g in range(G2)]
            lfs = [lf_ref[0, pl.ds(r0, C), :] for r0 in r0s]
            his = [lf.astype(BF16) for lf in lfs]
            mids = [(lf - hi.astype(F32)).astype(BF16) for lf, hi in zip(lfs, his)]
            rhs = jnp.concatenate([jnp.concatenate(his, axis=1), jnp.concatenate(mids, axis=1)], axis=0)
            ball = jnp.dot(tri_ref[...], rhs, preferred_element_type=F32)
            for g in range(G2):
                b_ref[pl.ds(r0s[g], C), :] = ball[:, g * W:(g + 1) * W]
            return carry

        lax.fori_loop(0, nchunk // G2, phase0, 0)

        def phase1(i, carry):
            cs = [i * G + g for g in range(G)]
            r0s = [pl.multiple_of(c * C, C) for c in cs]
            lfs = [lf_ref[0, pl.ds(r0, C), :] for r0 in r0s]
            qis, kls, kbars = [], [], []
            for g in range(G):
                b = b_ref[pl.ds(r0s[g], C), :]
                k = 1.0 - jnp.exp(lfs[g])
                q = q_ref[0, pl.ds(r0s[g], C), :].astype(F32)
                qi = (q * jnp.exp(b)).astype(BF16)
                qi_ref[pl.ds(r0s[g], C), :] = qi
                dec = jnp.exp(b[0:1] if rev else b[C - 1:C])
                dec_ref[pl.ds(pl.multiple_of(cs[g] * SUB, SUB), SUB), :] = jnp.broadcast_to(dec, (SUB, W))
                qis.append(qi)
                kbar = k * jnp.exp(-b)
                kls.append((kbar * dec).astype(BF16))
                kbars.append(kbar.astype(BF16))
            avals = [lax.dot_general(qis[g], _blockdiag2(kbars[g]), NT_DIMS, preferred_element_type=F32)
                     for g in range(G)]
            for g in range(G):
                v = v_ref[0, pl.ds(r0s[g], C), :]
                for j in range(HG_PACK):
                    ls = slice(j * HEAD, (j + 1) * HEAD)
                    upd_ref[pl.ds(pl.multiple_of(cs[g] * HEAD, HEAD), HEAD), ls] = lax.dot_general(
                        v[:, ls], kls[g][:, ls], TN_DIMS, preferred_element_type=F32)
            for g in range(G):
                a_ref[pl.ds(r0s[g], C), :] = jnp.where(causal, avals[g], 0.0).astype(BF16)
            return carry

        lax.fori_loop(0, nchunk // G, phase1, 0)

        def phase2(i, carry):
            c = (nchunk - 1 - i) if rev else i
            s0 = pl.multiple_of(c * HEAD, HEAD)
            st = st_ref[...]
            snap_ref[pl.ds(s0, HEAD), :] = st.astype(BF16)
            dec = dec_ref[pl.ds(pl.multiple_of(c * SUB, SUB), 1), :]
            st_ref[...] = st * dec + upd_ref[pl.ds(s0, HEAD), :]
            return carry

        lax.fori_loop(0, nchunk, phase2, 0, unroll=G)

        def phase3(i, carry):
            cs = [i * G + g for g in range(G)]
            r0s = [pl.multiple_of(c * C, C) for c in cs]
            snaps = [snap_ref[pl.ds(pl.multiple_of(c * HEAD, HEAD), HEAD), :] for c in cs]
            o1 = [lax.dot_general(qi_ref[pl.ds(r0s[g], C), :], _blockdiag2(snaps[g]), NT_DIMS,
                                  preferred_element_type=F32) for g in range(G)]
            o2 = [jnp.dot(a_ref[pl.ds(r0s[g], C), :], _blockdiag2(v_ref[0, pl.ds(r0s[g], C), :]),
                          preferred_element_type=F32) for g in range(G)]
            for g in range(G):
                emit(o1[g] + o2[g], r0s[g], slice(None))
            return carry

        lax.fori_loop(0, nchunk // G, phase3, 0)

    @pl.when(jnp.logical_not(safe))
    def _():
        def body(i, carry):
            c = (nchunk - 1 - i) if rev else i
            r0 = pl.multiple_of(c * C, C)
            for j in range(HG_PACK):
                ls = slice(j * HEAD, (j + 1) * HEAD)
                o = _hgrn_robust_chunk(r0, ls, q_ref, lf_ref, v_ref, tri_ref, st_ref, kc_ref, bc_ref, rev)
                emit(o, r0, ls)
            return carry

        lax.fori_loop(0, nchunk, body, 0)


def _tri_matrix(rev):
    t = np.arange(CHUNK)
    m = (t[None, :] >= t[:, None]) if rev else (t[None, :] <= t[:, None])
    return jnp.asarray(np.concatenate([m, m], axis=1).astype(np.float32), dtype=BF16)


def _chunk_sum_matrix(rows):
    nchunk = rows // CHUNK
    nsel = -(-nchunk // 16) * 16
    m = np.zeros((nsel, rows), np.float32)
    m[np.arange(rows) // CHUNK, np.arange(rows)] = 1.0
    return jnp.asarray(m, dtype=BF16)


def _hgrn_pass(a_silu, a_lf, a_plain, gnorm, o_fwd, rev, rows):
    bsz, s, _ = a_silu.shape
    rows = min(rows, s)
    nblk = s // rows
    nchunk = rows // CHUNK
    w = HG_PACK * HEAD
    npk = HG_HEADS // HG_PACK
    final = o_fwd is not None
    seq = (lambda i: nblk - 1 - i) if rev else (lambda i: i)
    fcol = npk if rev else 0
    sel = _chunk_sum_matrix(rows)
    in_specs = [
        pl.BlockSpec((1, rows, w), lambda b, h, i: (b, seq(i), h)),
        pl.BlockSpec((1, rows, w), lambda b, h, i: (b, seq(i), h + fcol)),
        pl.BlockSpec((1, rows, w), lambda b, h, i: (b, seq(i), h)),
        pl.BlockSpec((CHUNK, 2 * CHUNK), lambda b, h, i: (0, 0)),
        pl.BlockSpec(sel.shape, lambda b, h, i: (0, 0)),
    ]
    args = [a_silu, a_lf, a_plain, _tri_matrix(rev), sel]
    if final:
        in_specs += [
            pl.BlockSpec((1, rows, w), lambda b, h, i: (b, seq(i), h)),
            pl.BlockSpec((1, rows, w), lambda b, h, i: (b, seq(i), h + npk)),
            pl.BlockSpec((1, HEAD), lambda b, h, i: (0, 0)),
        ]
        args += [o_fwd, a_silu, gnorm.reshape(1, HEAD)]
    return pl.pallas_call(
        functools.partial(_hgrn_kernel, rev=rev, final=final, nchunk=nchunk),
        grid=(bsz, npk, nblk),
        in_specs=in_specs,
        out_specs=pl.BlockSpec((1, rows, w), lambda b, h, i: (b, seq(i), h)),
        out_shape=jax.ShapeDtypeStruct((bsz, s, HG_HEADS * HEAD), BF16 if final else F32),
        scratch_shapes=[
            pltpu.VMEM((HEAD, w), F32),
            pltpu.VMEM((rows, w), BF16),
            pltpu.VMEM((rows, 2 * CHUNK), BF16),
            pltpu.VMEM((nchunk * HEAD, w), F32),
            pltpu.VMEM((nchunk * HEAD, w), BF16),
            pltpu.VMEM((nchunk * SUB, w), F32),
            pltpu.VMEM((rows, w), F32),
            pltpu.VMEM((CHUNK + 2 * SUB, HEAD), F32),
            pltpu.VMEM((CHUNK + 2 * SUB, HEAD), F32),
        ],
        compiler_params=_cparams(("parallel", "parallel", "arbitrary")),
        name="hgrn_bwd" if rev else "hgrn_fwd",
    )(*args)


def _attn_group(q_ref, k_ref, v_ref, bias_ref, qs_ref, ks_ref, vs_ref, acc_ref, m_ref, l_ref, *, dil, span, first):
    s = q_ref.shape[1]
    length = s // dil
    qt, kt = 2 * span, 4 * span
    ntile = length // qt
    if dil > 1:
        qs_ref[...] = q_ref[0].astype(F32)
        ks_ref[...] = k_ref[0].astype(F32)
        vs_ref[...] = v_ref[0].astype(F32)

    G = min(ATTN_UNROLL if dil > 1 else 2 * ATTN_UNROLL, ntile)
    assert ntile % G == 0
    ngrp = ntile // G
    nstatic = min(dil, SUB)
    ndyn = dil // nstatic
    slack = SUB if ndyn > 1 else 0
    for rs in range(nstatic):
        def tiles(i, carry):
            js = [(i % ngrp) * G + g for g in range(G)]
            w0s = [jnp.clip(j * qt - span, 0, length - kt) for j in js]
            if ndyn > 1:
                roff = (i // ngrp) * SUB
                qbases = [pl.multiple_of(j * (qt * dil) + roff, SUB) for j in js]
                kbases = [pl.multiple_of(w0 * dil + roff, SUB) for w0 in w0s]
            else:
                qbases = [pl.multiple_of(j * (qt * dil), qt * dil) for j in js]
                kbases = [pl.multiple_of(w0 * dil, span * dil) for w0 in w0s]
            rows = pl.ds(rs, qt, stride=dil) if dil > 1 else slice(None)

            def window(ref, sref, base, n):
                if dil > 1:
                    return sref.at[pl.ds(base, n * dil - slack)][pl.ds(rs, n, stride=dil), :].astype(BF16)
                return ref[0, pl.ds(base, n), :]

            scs = [lax.dot_general(window(q_ref, qs_ref, qbases[g], qt), window(k_ref, ks_ref, kbases[g], kt),
                                   NT_DIMS, preferred_element_type=F32) for g in range(G)]
            ps, mns, lns = [], [], []
            for g in range(G):
                sel = (js[g] * qt - w0s[g]) // span
                sc = scs[g] + bias_ref[pl.ds(pl.multiple_of(sel * qt, qt), qt), :]
                mx = jnp.max(sc, axis=-1, keepdims=True)
                p = jnp.exp(sc - mx)
                lns.append(jnp.broadcast_to(jnp.sum(p, axis=-1, keepdims=True), (qt, HEAD)))
                mns.append(jnp.broadcast_to(mx, (qt, HEAD)))
                ps.append(p.astype(BF16))
            ons = [jnp.dot(ps[g], window(v_ref, vs_ref, kbases[g], kt), preferred_element_type=F32)
                   for g in range(G)]
            for g in range(G):
                mv = m_ref.at[pl.ds(qbases[g], qt * dil - slack)]
                lv = l_ref.at[pl.ds(qbases[g], qt * dil - slack)]
                av = acc_ref.at[pl.ds(qbases[g], qt * dil - slack)]
                if first:
                    mv[rows, :] = mns[g]
                    lv[rows, :] = lns[g]
                    av[rows, :] = ons[g]
                else:
                    mo = mv[rows, :]
                    mm = jnp.maximum(mo, mns[g])
                    wo = jnp.exp(mo - mm)
                    wn = jnp.exp(mns[g] - mm)
                    mv[rows, :] = mm
                    lv[rows, :] = wo * lv[rows, :] + wn * lns[g]
                    av[rows, :] = wo * av[rows, :] + wn * ons[g]
            return carry

        lax.fori_loop(0, ndyn * ngrp, tiles, 0)


def _attn_kernel(q_ref, k_ref, v_ref, bias_ref, o_ref, qs_ref, ks_ref, vs_ref, acc_ref, m_ref, l_ref):
    g = pl.program_id(2)
    for gi, (win, dil) in enumerate(ATTN_GROUPS):
        @pl.when(g == gi)
        def _(gi=gi, win=win, dil=dil):
            _attn_group(q_ref, k_ref, v_ref, bias_ref, qs_ref, ks_ref, vs_ref, acc_ref, m_ref, l_ref,
                        dil=dil, span=(win // 2) // dil, first=(gi == 0))

    @pl.when(g == len(ATTN_GROUPS) - 1)
    def _():
        o_ref[0] = (acc_ref[...] * (1.0 / l_ref[...])).astype(o_ref.dtype)


def _band_bias(span):
    qt, kt = 2 * span, 4 * span
    row = np.arange(qt)[:, None]
    col = np.arange(kt)[None, :]
    blocks = [np.where(np.abs(col - k * span - row) <= span, 0.0, NEG) for k in range(3)]
    return jnp.asarray(np.concatenate(blocks, axis=0), dtype=F32)


def _dilated_attention(a_rot, a_plain):
    bsz, s, _ = a_rot.shape
    ng = len(ATTN_GROUPS)
    spans = {(win // 2) // dil for win, dil in ATTN_GROUPS}
    assert len(spans) == 1, "one band-mask table serves every group"
    bias = _band_bias(spans.pop())
    vbase = HG_HEADS
    blk = (1, s, HEAD)
    return pl.pallas_call(
        _attn_kernel,
        grid=(bsz, ATTN_HEADS, ng),
        in_specs=[pl.BlockSpec(blk, lambda b, h, g: (b, 0, 2 * g * ATTN_HEADS + h)),
                  pl.BlockSpec(blk, lambda b, h, g: (b, 0, (2 * g + 1) * ATTN_HEADS + h)),
                  pl.BlockSpec(blk, lambda b, h, g: (b, 0, vbase + g * ATTN_HEADS + h)),
                  pl.BlockSpec(bias.shape, lambda b, h, g: (0, 0))],
        out_specs=pl.BlockSpec(blk, lambda b, h, g: (b, 0, h)),
        out_shape=jax.ShapeDtypeStruct((bsz, s, ATTN_WIDTH), BF16),
        scratch_shapes=[pltpu.VMEM((s, HEAD), F32)] * 6,
        compiler_params=_cparams(("parallel", "parallel", "arbitrary")),
        name="dilated_attn",
    )(a_rot, a_rot, a_plain, bias)


def _merge_xattn_kernel(*refs, bounds):
    nsrc = len(bounds) - 1
    (hg_ref, attn_ref, gate_ref, who_ref, wao_ref, wout_ref,
     g_ref, wq_ref, kv_ref, wo_ref, out_ref) = refs[nsrc:]
    x = _pick_source(pl.program_id(0), refs[:nsrc], bounds)
    d = x.shape[1]
    hd = d // XA_HEADS
    yh = jnp.dot(hg_ref[...], who_ref[...], preferred_element_type=F32)
    ya = jnp.dot(attn_ref[...], wao_ref[...], preferred_element_type=F32)
    merged = gate_ref[:, :d].astype(F32) * yh + gate_ref[:, d:].astype(F32) * ya
    hx = x + jnp.dot(merged.astype(BF16), wout_ref[...], preferred_element_type=F32)
    u = (hx * _rms_scale(hx) * g_ref[...]).astype(BF16)
    q = (jnp.dot(u, wq_ref[...], preferred_element_type=F32) * (hd ** -0.5)).astype(BF16)
    outs = []
    for h in range(XA_HEADS):
        kh = kv_ref[0, :, h * hd:(h + 1) * hd]
        vh = kv_ref[0, :, d + h * hd:d + (h + 1) * hd]
        s = lax.dot_general(q[:, h * hd:(h + 1) * hd], kh, NT_DIMS, preferred_element_type=F32)
        p = jnp.exp(s - jnp.max(s, axis=-1, keepdims=True))
        den = jnp.sum(p, axis=-1, keepdims=True)
        outs.append(jnp.dot(p.astype(BF16), vh, preferred_element_type=F32) * (1.0 / den))
    o = jnp.concatenate(outs, axis=1).astype(BF16)
    out_ref[...] = hx + jnp.dot(o, wo_ref[...], preferred_element_type=F32)


def _merge_xattn(srcs, hg, attn, gates, who, wao, wout, g, wq, kv, wo, seq_len, tm=512):
    d = srcs[0].shape[1]
    xspecs, bounds = _row_sources(srcs, tm)
    tiles_per_seq = seq_len // tm
    row = lambda w: pl.BlockSpec((tm, w), lambda i: (i, 0))
    full = lambda a: pl.BlockSpec(a.shape, lambda i: (0, 0))
    return pl.pallas_call(
        functools.partial(_merge_xattn_kernel, bounds=bounds),
        grid=(bounds[-1],),
        in_specs=xspecs + [row(d), row(ATTN_WIDTH), row(2 * d), full(who), full(wao), full(wout),
                           pl.BlockSpec((1, d), lambda i: (0, 0)), full(wq),
                           pl.BlockSpec((1,) + kv.shape[1:], lambda i: (i // tiles_per_seq, 0, 0)), full(wo)],
        out_specs=row(d),
        out_shape=jax.ShapeDtypeStruct((bounds[-1] * tm, d), F32),
        compiler_params=_cparams(("parallel",)),
        name="merge_xattn",
    )(*srcs, hg, attn, gates, who, wao, wout, g.reshape(1, d), wq, kv, wo)


def _mlp_kernel(h_ref, g_ref, w1_ref, w2_ref, gf_ref, *out_refs, bounds):
    hx = h_ref[...]
    u = (hx * _rms_scale(hx) * g_ref[...]).astype(BF16)
    a = jnp.maximum(jnp.dot(u, w1_ref[...], preferred_element_type=F32), 0.0)
    a = (a * a).astype(BF16)
    y = hx + jnp.dot(a, w2_ref[...], preferred_element_type=F32)
    y = y * _rms_scale(y) * gf_ref[...]
    i = pl.program_id(0)
    for k, out_ref in enumerate(out_refs):
        @pl.when(jnp.logical_and(i >= bounds[k], i < bounds[k + 1]))
        def _(out_ref=out_ref):
            out_ref[...] = y


def _mlp(h2d, g, w1, w2, gf, group_rows, tm=512):
    m, d = h2d.shape
    assert sum(group_rows) == m
    bounds = [0]
    for r in group_rows:
        assert r % tm == 0
        bounds.append(bounds[-1] + r // tm)
    full = lambda a: pl.BlockSpec(a.shape, lambda i: (0, 0))
    out_specs = [pl.BlockSpec((tm, d), lambda i, lo=bounds[k], n=bounds[k + 1] - bounds[k]:
                              (jnp.clip(i - lo, 0, n - 1), 0)) for k in range(len(group_rows))]
    return pl.pallas_call(
        functools.partial(_mlp_kernel, bounds=bounds),
        grid=(m // tm,),
        in_specs=[pl.BlockSpec((tm, d), lambda i: (i, 0)), pl.BlockSpec((1, d), lambda i: (0, 0)),
                  full(w1), full(w2), pl.BlockSpec((1, d), lambda i: (0, 0))],
        out_specs=out_specs,
        out_shape=[jax.ShapeDtypeStruct((r, d), F32) for r in group_rows],
        compiler_params=_cparams(("arbitrary",)),
        name="mlp_final",
    )(h2d, g.reshape(1, d), w1, w2, gf.reshape(1, d))


def _rotary_tables(s):
    half = ROT_DIM // 2
    inv = ROPE_THETA ** (-jnp.arange(half, dtype=F32) * 2.0 / ROT_DIM)
    ang = jnp.arange(s, dtype=F32)[:, None] * inv[None, :]
    cos, sin = jnp.cos(ang), jnp.sin(ang)
    pad = jnp.zeros((s, HEAD - ROT_DIM), F32)
    zero = jnp.zeros((s, half), F32)
    cos_t = jnp.concatenate([cos, cos, pad + 1.0], axis=1)
    sin_lo = jnp.concatenate([-sin, zero, pad], axis=1)
    sin_hi = jnp.concatenate([zero, sin, pad], axis=1)
    return cos_t, sin_lo, sin_hi


def _encode(xs, mems, mix_norm_g, w_in, hgrn_lb_logits, hgrn_gnorm_g, w_hgrn_o, w_attn_o, w_out,
            xa_norm_g, mem_norm_g, w_xq, w_xkv, w_xo, ffn_norm_g, w_ffn1, w_ffn2, final_norm_g,
            hgrn_rows=2048):
    s, d = xs[0].shape[1:]
    bsz = sum(x.shape[0] for x in xs)
    t = bsz * s
    depth = w_in.shape[0]
    assert depth == 1, "the final norm is fused into the (single) layer's MLP call"
    l = 0
    fd = HG_HEADS * HEAD
    lb_all = jnp.cumsum(jax.nn.softmax(hgrn_lb_logits.astype(F32), axis=1), axis=1)
    sizes = (fd,) * 5 + (ATTN_WIDTH,) * 9 + (d, d)
    offs = np.concatenate([[0], np.cumsum(sizes)])
    cos_t, sin_lo, sin_hi = _rotary_tables(s)
    tm = min(TOKEN_TILE, s)
    tmp = min(PROJ_TOKEN_TILE, s)
    tabspec = pl.BlockSpec((tmp, HEAD), lambda i: (i % (s // tmp), 0))
    srcs = [x.reshape(-1, d) for x in xs]
    mem = jnp.concatenate(mems, axis=0) if len(mems) > 1 else mems[0]

    wl = w_in[l]
    seg = lambda p: wl[:, offs[p]:offs[p + 1]]
    bf = lambda a: a.astype(BF16)
    w_silu = bf(jnp.concatenate([seg(0), seg(4)], axis=1))
    w_lf = bf(jnp.concatenate([seg(1), seg(2)], axis=1))
    w_plain = bf(jnp.concatenate([seg(3), seg(7), seg(10), seg(13)], axis=1))
    w_rot = bf(jnp.concatenate([seg(5), seg(6), seg(8), seg(9), seg(11), seg(12)], axis=1))
    w_gate = bf(jnp.concatenate([seg(14), seg(15)], axis=1))
    g_mix = mix_norm_g[l]

    a_silu, u = _first_proj(srcs, g_mix, w_silu, _ep_silu, BF16, tmp, "proj_silu")
    proj = functools.partial(_norm_proj, u, g_mix, tm=tmp, norm=False)
    lb_row = lb_all[:, l].reshape(1, 2 * fd)
    a_lf = proj(w_lf, _ep_logf, F32, extras=(lb_row,),
                extra_specs=(pl.BlockSpec((1, 2 * fd), lambda i: (0, 0)),), name="proj_logf")
    a_plain = proj(w_plain, _ep_plain, BF16, name="proj_plain")
    qscale = jnp.tile(jnp.concatenate([jnp.full((ATTN_WIDTH,), HEAD ** -0.5, F32),
                                       jnp.ones((ATTN_WIDTH,), F32)]), 3).reshape(1, 6 * ATTN_WIDTH)
    a_rot = proj(w_rot, _ep_rotary, BF16, extras=(cos_t, sin_lo, sin_hi, qscale),
                 extra_specs=(tabspec, tabspec, tabspec, pl.BlockSpec((1, 6 * ATTN_WIDTH), lambda i: (0, 0))),
                 name="proj_rotary")
    a_gate = proj(w_gate, _ep_sigmoid, BF16, name="proj_gate")

    a_silu3 = a_silu.reshape(bsz, s, 2 * fd)
    a_lf3 = a_lf.reshape(bsz, s, 2 * fd)
    a_plain3 = a_plain.reshape(bsz, s, -1)
    o_fwd = _hgrn_pass(a_silu3, a_lf3, a_plain3, hgrn_gnorm_g[l], None, False, hgrn_rows)
    hg = _hgrn_pass(a_silu3, a_lf3, a_plain3, hgrn_gnorm_g[l], o_fwd, True, hgrn_rows)

    attn = _dilated_attention(a_rot.reshape(bsz, s, -1), a_plain3)

    nm = mem.shape[1]
    kv = _norm_proj(mem.reshape(bsz * nm, d), mem_norm_g[l], w_xkv[l].astype(BF16), _ep_plain, BF16,
                    tm=nm, name="proj_memkv")
    h2d = _merge_xattn(srcs, hg.reshape(t, fd), attn.reshape(t, ATTN_WIDTH), a_gate,
                       w_hgrn_o[l].astype(BF16), w_attn_o[l].astype(BF16), w_out[l].astype(BF16),
                       xa_norm_g[l], w_xq[l].astype(BF16), kv.reshape(bsz, nm, 2 * d), w_xo[l].astype(BF16),
                       s, tm=tm)
    outs = _mlp(h2d, ffn_norm_g[l], w_ffn1[l].astype(BF16), w_ffn2[l].astype(BF16), final_norm_g,
                [x.shape[0] * s for x in xs], tm=tm)
    return [o.reshape(x.shape) for o, x in zip(outs, xs)]


def kernel(x_prompt, x_sample, mem_prompt, mem_sample, mix_norm_g, w_in, hgrn_lb_logits, hgrn_gnorm_g,
           w_hgrn_o, w_attn_o, w_out, xa_norm_g, mem_norm_g, w_xq, w_xkv, w_xo, ffn_norm_g, w_ffn1,
           w_ffn2, final_norm_g):
    assert x_prompt.shape[1:] == x_sample.shape[1:]
    y_prompt, y_sample = _encode(
        [x_prompt, x_sample], [mem_prompt, mem_sample], mix_norm_g, w_in, hgrn_lb_logits, hgrn_gnorm_g,
        w_hgrn_o, w_attn_o, w_out, xa_norm_g, mem_norm_g, w_xq, w_xkv, w_xo, ffn_norm_g, w_ffn1, w_ffn2,
        final_norm_g)
    return y_prompt, y_sample
```

```python
import functools

import numpy as np
import jax
import jax.numpy as jnp
from jax import lax
from jax.experimental import pallas as pl
from jax.experimental.pallas import tpu as pltpu

F32 = jnp.float32
BF16 = jnp.bfloat16

RMS_EPS = 1e-6
ROPE_THETA = 500000.0
HG_HEADS = 8
HEAD = 128
ATTN_HEADS = 4
ATTN_WIDTH = ATTN_HEADS * HEAD
ATTN_GROUPS = ((128, 1), (512, 4), (2048, 16))
ROT_DIM = HEAD // 4
XA_HEADS = 4
CHUNK = 64
SUB = 8
NEG = -1e30
HG_PACK = 2
SAFE_CHUNK_LOG_DECAY = -60.0
HGRN_UNROLL = 16
ATTN_UNROLL = 4
ATTN_SEGMENT = 1024
NT_DIMS = (((1,), (1,)), ((), ()))
TN_DIMS = (((0,), (0,)), ((), ()))

TOKEN_TILE = 512
PROJ_TOKEN_TILE = 1024

VMEM_LIMIT = 56 * 1024 * 1024


def _cparams(sem):
    return pltpu.CompilerParams(dimension_semantics=sem, vmem_limit_bytes=VMEM_LIMIT)


def _sigmoid(x):
    return 1.0 / (1.0 + jnp.exp(-x))


def _rms_scale(xf):
    return lax.rsqrt(jnp.mean(xf * xf, axis=-1, keepdims=True) + RMS_EPS)


def _ep_plain(acc):
    return acc


def _sigmoid_tanh(x):
    return 0.5 * jnp.tanh(0.5 * x) + 0.5


def _ep_silu(acc):
    return acc * _sigmoid_tanh(acc)


def _ep_sigmoid(acc):
    return _sigmoid_tanh(acc)


def _ep_logf(acc, lb):
    return jnp.log(lb + (1.0 - lb) * _sigmoid(acc))


def _ep_rotary(acc, cos, sin_lo, sin_hi, colscale):
    half = ROT_DIM // 2
    outs = []
    for c in range(acc.shape[1] // HEAD):
        t = acc[:, c * HEAD:(c + 1) * HEAD]
        r = t * cos + pltpu.roll(t, HEAD - half, 1) * sin_lo + pltpu.roll(t, half, 1) * sin_hi
        outs.append(r)
    return jnp.concatenate(outs, axis=1) * colscale


def _row_sources(srcs, tm):
    bounds = [0]
    for a in srcs:
        assert a.shape[0] % tm == 0
        bounds.append(bounds[-1] + a.shape[0] // tm)
    specs = [pl.BlockSpec((tm, a.shape[1]),
                          lambda i, lo=bounds[k], n=bounds[k + 1] - bounds[k]: (jnp.clip(i - lo, 0, n - 1), 0))
             for k, a in enumerate(srcs)]
    return specs, bounds


def _pick_source(i, refs, bounds):
    x = refs[-1][...]
    for k in range(len(refs) - 2, -1, -1):
        x = jnp.where(i < bounds[k + 1], refs[k][...], x)
    return x


def _first_proj_kernel(*refs, epilogue, bounds):
    nsrc = len(bounds) - 1
    g_ref, w_ref, o_ref, u_ref = refs[nsrc:]
    xf = _pick_source(pl.program_id(0), refs[:nsrc], bounds)
    u = (xf * _rms_scale(xf) * g_ref[...]).astype(BF16)
    u_ref[...] = u
    o_ref[...] = epilogue(jnp.dot(u, w_ref[...], preferred_element_type=F32)).astype(o_ref.dtype)


def _first_proj(srcs, g, w, epilogue, out_dtype, tm, name):
    d = srcs[0].shape[1]
    n = w.shape[1]
    specs, bounds = _row_sources(srcs, tm)
    m = bounds[-1] * tm
    return pl.pallas_call(
        functools.partial(_first_proj_kernel, epilogue=epilogue, bounds=bounds),
        grid=(bounds[-1],),
        in_specs=specs + [pl.BlockSpec((1, d), lambda i: (0, 0)), pl.BlockSpec((d, n), lambda i: (0, 0))],
        out_specs=[pl.BlockSpec((tm, n), lambda i: (i, 0)), pl.BlockSpec((tm, d), lambda i: (i, 0))],
        out_shape=[jax.ShapeDtypeStruct((m, n), out_dtype), jax.ShapeDtypeStruct((m, d), BF16)],
        compiler_params=_cparams(("parallel",)),
        name=name,
    )(*srcs, g.reshape(1, d), w)


def _norm_proj_kernel(*refs, epilogue, n_extra, norm):
    x_ref, g_ref, w_ref = refs[:3]
    extra = refs[3:3 + n_extra]
    o_ref = refs[3 + n_extra]
    if norm:
        xf = x_ref[...]
        u = (xf * _rms_scale(xf) * g_ref[...]).astype(BF16)
    else:
        u = x_ref[...]
    acc = jnp.dot(u, w_ref[...], preferred_element_type=F32)
    o_ref[...] = epilogue(acc, *[e[...] for e in extra]).astype(o_ref.dtype)


def _norm_proj(x2d, g, w, epilogue, out_dtype, extras=(), extra_specs=(), tm=512, name="norm_proj", norm=True):
    m, d = x2d.shape
    n = w.shape[1]
    tm = min(tm, m)
    assert m % tm == 0
    in_specs = [
        pl.BlockSpec((tm, d), lambda i: (i, 0)),
        pl.BlockSpec((1, d), lambda i: (0, 0)),
        pl.BlockSpec((d, n), lambda i: (0, 0)),
    ] + list(extra_specs)
    return pl.pallas_call(
        functools.partial(_norm_proj_kernel, epilogue=epilogue, n_extra=len(extras), norm=norm),
        grid=(m // tm,),
        in_specs=in_specs,
        out_specs=pl.BlockSpec((tm, n), lambda i: (i, 0)),
        out_shape=jax.ShapeDtypeStruct((m, n), out_dtype),
        compiler_params=_cparams(("parallel",)),
        name=name,
    )(x2d, g.reshape(1, d), w, *extras)


def _hgrn_robust_chunk(r0, ls, q_ref, lf_ref, v_ref, tri_ref, st_ref, kc_ref, bc_ref, rev):
    C = CHUNK
    ng = C // SUB
    rowl = lax.broadcasted_iota(jnp.int32, (C, HEAD), 0)
    sub_pos = rowl & (SUB - 1)
    arow = lax.broadcasted_iota(jnp.int32, (C, C), 0)
    acol = lax.broadcasted_iota(jnp.int32, (C, C), 1)
    ones_b = jnp.ones((HEAD, HEAD), BF16)
    zeros_g = jnp.zeros((SUB, HEAD), F32)

    lf = lf_ref[0, pl.ds(r0, C), ls]
    q = q_ref[0, pl.ds(r0, C), ls].astype(F32)
    v = v_ref[0, pl.ds(r0, C), ls]
    k = 1.0 - jnp.exp(lf)
    hi = lf.astype(BF16)
    mid = (lf - hi.astype(F32)).astype(BF16)
    b = jnp.dot(tri_ref[...], jnp.concatenate([hi, mid], axis=0), preferred_element_type=F32)
    kc_ref[pl.ds(SUB, C), :] = k
    bc_ref[pl.ds(SUB, C), :] = b
    b_end = bc_ref[pl.ds(SUB + (0 if rev else C - 1), 1), :]

    st = st_ref[:, ls]
    qi = (q * jnp.exp(b)).astype(BF16)
    o = lax.dot_general(qi, st.astype(BF16), NT_DIMS, preferred_element_type=F32)
    kl = (k * jnp.exp(b_end - b)).astype(BF16)
    st_ref[:, ls] = st * jnp.exp(b_end) + lax.dot_general(v, kl, TN_DIMS, preferred_element_type=F32)

    a = jnp.zeros((C, C), F32)
    h = C // 2
    while h >= SUB:
        qparts, kparts = [], []
        for gi in range(ng):
            t0 = gi * SUB
            blk = t0 // (2 * h)
            in_upper = (t0 % (2 * h)) >= h
            is_query = (not in_upper) if rev else in_upper
            rr = blk * 2 * h + (h if rev else h - 1)
            bref = bc_ref[pl.ds(SUB + rr, 1), :]
            bg = b[t0:t0 + SUB]
            if is_query:
                qparts.append(q[t0:t0 + SUB] * jnp.exp(bg - bref))
                kparts.append(zeros_g)
            else:
                qparts.append(zeros_g)
                kparts.append(k[t0:t0 + SUB] * jnp.exp(bref - bg))
        qh = jnp.concatenate(qparts, axis=0).astype(BF16)
        kh = jnp.concatenate(kparts, axis=0).astype(BF16)
        ah = lax.dot_general(qh, kh, NT_DIMS, preferred_element_type=F32)
        if 2 * h < C:
            ah = jnp.where((arow ^ acol) < 2 * h, ah, 0.0)
        a = a + ah
        h //= 2

    for d in range(SUB):
        sh = SUB + (d if rev else -d)
        ks = kc_ref[pl.ds(sh, C), :]
        bs = bc_ref[pl.ds(sh, C), :]
        ok = (sub_pos + d <= SUB - 1) if rev else (sub_pos >= d)
        p = jnp.where(ok, q * ks * jnp.exp(b - bs), 0.0).astype(BF16)
        rs = jnp.dot(p, ones_b, preferred_element_type=F32)[:, :C]
        tgt = (arow + d) if rev else (arow - d)
        a = a + jnp.where(acol == tgt, rs, 0.0)

    return o + jnp.dot(a.astype(BF16), v, preferred_element_type=F32)


def _blockdiag2(x):
    z = jnp.zeros((x.shape[0], HEAD), x.dtype)
    return jnp.concatenate([jnp.concatenate([x[:, :HEAD], z], axis=1),
                            jnp.concatenate([z, x[:, HEAD:]], axis=1)], axis=0)


def _hgrn_kernel(*refs, rev, final, nchunk):
    if final:
        (q_ref, lf_ref, v_ref, tri_ref, sel_ref, of_ref, g_ref, gn_ref, o_ref,
         st_ref, qi_ref, a_ref, upd_ref, snap_ref, dec_ref, b_ref, kc_ref, bc_ref) = refs
    else:
        (q_ref, lf_ref, v_ref, tri_ref, sel_ref, o_ref,
         st_ref, qi_ref, a_ref, upd_ref, snap_ref, dec_ref, b_ref, kc_ref, bc_ref) = refs
        of_ref = g_ref = gn_ref = None
    C = CHUNK
    W = HG_PACK * HEAD

    @pl.when(pl.program_id(2) == 0)
    def _():
        st_ref[...] = jnp.zeros_like(st_ref)
        kc_ref[...] = jnp.zeros_like(kc_ref)
        bc_ref[...] = jnp.zeros_like(bc_ref)

    def emit(o, r0, ls):
        if final:
            tot = o + of_ref[0, pl.ds(r0, C), ls]
            parts = []
            for j in range(tot.shape[1] // HEAD):
                tj = tot[:, j * HEAD:(j + 1) * HEAD]
                parts.append(tj * _rms_scale(tj) * gn_ref[...])
            y = parts[0] if len(parts) == 1 else jnp.concatenate(parts, axis=1)
            y = y * g_ref[0, pl.ds(r0, C), ls].astype(F32)
            o_ref[0, pl.ds(r0, C), ls] = y.astype(o_ref.dtype)
        else:
            o_ref[0, pl.ds(r0, C), ls] = o

    sums = jnp.dot(sel_ref[...], lf_ref[0].astype(BF16), preferred_element_type=F32)
    safe = jnp.min(sums) >= SAFE_CHUNK_LOG_DECAY

    @pl.when(safe)
    def _():
        arow = lax.broadcasted_iota(jnp.int32, (C, 2 * C), 0)
        acol = lax.broadcasted_iota(jnp.int32, (C, 2 * C), 1) & (C - 1)
        causal = (acol >= arow) if rev else (acol <= arow)

        G = HGRN_UNROLL
        assert nchunk % G == 0

        G2 = 2 * G
        assert nchunk % G2 == 0

        def phase0(i, carry):
            r0s = [pl.multiple_of((i * G2 + g) * C, C) for g in range(G2)]
            lfs = [lf_ref[0, pl.ds(r0, C), :] for r0 in r0s]
            his = [lf.astype(BF16) for lf in lfs]
            mids = [(lf - hi.astype(F32)).astype(BF16) for lf, hi in zip(lfs, his)]
            rhs = jnp.concatenate([jnp.concatenate(his, axis=1), jnp.concatenate(mids, axis=1)], axis=0)
            ball = jnp.dot(tri_ref[...], rhs, preferred_element_type=F32)
            for g in range(G2):
                b_ref[pl.ds(r0s[g], C), :] = ball[:, g * W:(g + 1) * W]
            return carry

        lax.fori_loop(0, nchunk // G2, phase0, 0)

        def phase1(i, carry):
            cs = [i * G + g for g in range(G)]
            r0s = [pl.multiple_of(c * C, C) for c in cs]
            lfs = [lf_ref[0, pl.ds(r0, C), :] for r0 in r0s]
            qis, kls, kbars = [], [], []
            for g in range(G):
                b = b_ref[pl.ds(r0s[g], C), :]
                k = 1.0 - jnp.exp(lfs[g])
                q = q_ref[0, pl.ds(r0s[g], C), :].astype(F32)
                qi = (q * jnp.exp(b)).astype(BF16)
                qi_ref[pl.ds(r0s[g], C), :] = qi
                dec = jnp.exp(b[0:1] if rev else b[C - 1:C])
                dec_ref[pl.ds(pl.multiple_of(cs[g] * SUB, SUB), SUB), :] = jnp.broadcast_to(dec, (SUB, W))
                qis.append(qi)
                kbar = k * jnp.exp(-b)
                kls.append((kbar * dec).astype(BF16))
                kbars.append(kbar.astype(BF16))
            avals = [lax.dot_general(qis[g], _blockdiag2(kbars[g]), NT_DIMS, preferred_element_type=F32)
                     for g in range(G)]
            for g in range(G):
                v = v_ref[0, pl.ds(r0s[g], C), :]
                for j in range(HG_PACK):
                    ls = slice(j * HEAD, (j + 1) * HEAD)
                    upd_ref[pl.ds(pl.multiple_of(cs[g] * HEAD, HEAD), HEAD), ls] = lax.dot_general(
                        v[:, ls], kls[g][:, ls], TN_DIMS, preferred_element_type=F32)
            for g in range(G):
                a_ref[pl.ds(r0s[g], C), :] = jnp.where(causal, avals[g], 0.0).astype(BF16)
            return carry

        lax.fori_loop(0, nchunk // G, phase1, 0)

        def phase2(i, carry):
            c = (nchunk - 1 - i) if rev else i
            s0 = pl.multiple_of(c * HEAD, HEAD)
            st = st_ref[...]
            snap_ref[pl.ds(s0, HEAD), :] = st.astype(BF16)
            dec = dec_ref[pl.ds(pl.multiple_of(c * SUB, SUB), 1), :]
            st_ref[...] = st * dec + upd_ref[pl.ds(s0, HEAD), :]
            return carry

        lax.fori_loop(0, nchunk, phase2, 0, unroll=G)

        def phase3(i, carry):
            cs = [i * G + g for g in range(G)]
            r0s = [pl.multiple_of(c * C, C) for c in cs]
            snaps = [snap_ref[pl.ds(pl.multiple_of(c * HEAD, HEAD), HEAD), :] for c in cs]
            o1 = [lax.dot_general(qi_ref[pl.ds(r0s[g], C), :], _blockdiag2(snaps[g]), NT_DIMS,
                                  preferred_element_type=F32) for g in range(G)]
            o2 = [jnp.dot(a_ref[pl.ds(r0s[g], C), :], _blockdiag2(v_ref[0, pl.ds(r0s[g], C), :]),
                          preferred_element_type=F32) for g in range(G)]
            for g in range(G):
                emit(o1[g] + o2[g], r0s[g], slice(None))
            return carry

        lax.fori_loop(0, nchunk // G, phase3, 0)

    @pl.when(jnp.logical_not(safe))
    def _():
        def body(i, carry):
            c = (nchunk - 1 - i) if rev else i
            r0 = pl.multiple_of(c * C, C)
            for j in range(HG_PACK):
                ls = slice(j * HEAD, (j + 1) * HEAD)
                o = _hgrn_robust_chunk(r0, ls, q_ref, lf_ref, v_ref, tri_ref, st_ref, kc_ref, bc_ref, rev)
                emit(o, r0, ls)
            return carry

        lax.fori_loop(0, nchunk, body, 0)


def _tri_matrix(rev):
    t = np.arange(CHUNK)
    m = (t[None, :] >= t[:, None]) if rev else (t[None, :] <= t[:, None])
    return jnp.asarray(np.concatenate([m, m], axis=1).astype(np.float32), dtype=BF16)


def _chunk_sum_matrix(rows):
    nchunk = rows // CHUNK
    nsel = -(-nchunk // 16) * 16
    m = np.zeros((nsel, rows), np.float32)
    m[np.arange(rows) // CHUNK, np.arange(rows)] = 1.0
    return jnp.asarray(m, dtype=BF16)


def _hgrn_pass(a_silu, a_lf, a_plain, gnorm, o_fwd, rev, rows):
    bsz, s, _ = a_silu.shape
    rows = min(rows, s)
    nblk = s // rows
    nchunk = rows // CHUNK
    w = HG_PACK * HEAD
    npk = HG_HEADS // HG_PACK
    final = o_fwd is not None
    seq = (lambda i: nblk - 1 - i) if rev else (lambda i: i)
    fcol = npk if rev else 0
    sel = _chunk_sum_matrix(rows)
    in_specs = [
        pl.BlockSpec((1, rows, w), lambda b, h, i: (b, seq(i), h)),
        pl.BlockSpec((1, rows, w), lambda b, h, i: (b, seq(i), h + fcol)),
        pl.BlockSpec((1, rows, w), lambda b, h, i: (b, seq(i), h)),
        pl.BlockSpec((CHUNK, 2 * CHUNK), lambda b, h, i: (0, 0)),
        pl.BlockSpec(sel.shape, lambda b, h, i: (0, 0)),
    ]
    args = [a_silu, a_lf, a_plain, _tri_matrix(rev), sel]
    if final:
        in_specs += [
            pl.BlockSpec((1, rows, w), lambda b, h, i: (b, seq(i), h)),
            pl.BlockSpec((1, rows, w), lambda b, h, i: (b, seq(i), h + npk)),
            pl.BlockSpec((1, HEAD), lambda b, h, i: (0, 0)),
        ]
        args += [o_fwd, a_silu, gnorm.reshape(1, HEAD)]
    return pl.pallas_call(
        functools.partial(_hgrn_kernel, rev=rev, final=final, nchunk=nchunk),
        grid=(bsz, npk, nblk),
        in_specs=in_specs,
        out_specs=pl.BlockSpec((1, rows, w), lambda b, h, i: (b, seq(i), h)),
        out_shape=jax.ShapeDtypeStruct((bsz, s, HG_HEADS * HEAD), BF16 if final else F32),
        scratch_shapes=[
            pltpu.VMEM((HEAD, w), F32),
            pltpu.VMEM((rows, w), BF16),
            pltpu.VMEM((rows, 2 * CHUNK), BF16),
            pltpu.VMEM((nchunk * HEAD, w), F32),
            pltpu.VMEM((nchunk * HEAD, w), BF16),
            pltpu.VMEM((nchunk * SUB, w), F32),
            pltpu.VMEM((rows, w), F32),
            pltpu.VMEM((CHUNK + 2 * SUB, HEAD), F32),
            pltpu.VMEM((CHUNK + 2 * SUB, HEAD), F32),
        ],
        compiler_params=_cparams(("parallel", "parallel", "arbitrary")),
        name="hgrn_bwd" if rev else "hgrn_fwd",
    )(*args)


def _attn_group(q_ref, k_ref, v_ref, bias_ref, qs_ref, ks_ref, vs_ref, tmp_ref,
                acc_ref, m_ref, l_ref, *, dil, span, first):
    s = q_ref.shape[1]
    length = s // dil
    qt, kt = 2 * span, 4 * span
    ntile = length // qt
    G = min(ATTN_UNROLL if dil > 1 else 2 * ATTN_UNROLL, ntile)
    assert ntile % G == 0
    ngrp = ntile // G
    nstatic = min(dil, SUB)
    ndyn = dil // nstatic
    slack = SUB if ndyn > 1 else 0
    resmajor = ndyn > 1
    if resmajor:
        nseg = tmp_ref.shape[0] // dil
        for src, dst in ((q_ref, qs_ref), (k_ref, ks_ref), (v_ref, vs_ref)):
            def segment(c, carry, src=src, dst=dst):
                n0 = pl.multiple_of(c * (nseg * dil), nseg * dil)
                tmp_ref[...] = src[0, pl.ds(n0, nseg * dil), :].astype(F32)
                for r in range(dil):
                    dst[pl.ds(pl.multiple_of(r * length + c * nseg, nseg), nseg), :] = (
                        tmp_ref[pl.ds(r, nseg, stride=dil), :])
                return carry

            lax.fori_loop(0, length // nseg, segment, 0)
    elif dil > 1:
        qs_ref[...] = q_ref[0].astype(F32)
        ks_ref[...] = k_ref[0].astype(F32)
        vs_ref[...] = v_ref[0].astype(F32)

    for rs in range(nstatic):
        def tiles(i, carry):
            js = [(i % ngrp) * G + g for g in range(G)]
            w0s = [jnp.clip(j * qt - span, 0, length - kt) for j in js]
            if ndyn > 1:
                rd = i // ngrp
                qbases = [pl.multiple_of(j * (qt * dil) + rd * SUB, SUB) for j in js]
                kbases = [pl.multiple_of(w0 * dil + rd * SUB, SUB) for w0 in w0s]
                res0 = (rs + rd * SUB) * length
            else:
                qbases = [pl.multiple_of(j * (qt * dil), qt * dil) for j in js]
                kbases = [pl.multiple_of(w0 * dil, span * dil) for w0 in w0s]
            rows = pl.ds(rs, qt, stride=dil) if dil > 1 else slice(None)

            def window(ref, sref, base, first_row, n):
                if resmajor:
                    return sref[pl.ds(pl.multiple_of(res0 + first_row, span), n), :].astype(BF16)
                if dil > 1:
                    return sref.at[pl.ds(base, n * dil - slack)][pl.ds(rs, n, stride=dil), :].astype(BF16)
                return ref[0, pl.ds(base, n), :]

            scs = [lax.dot_general(window(q_ref, qs_ref, qbases[g], js[g] * qt, qt),
                                   window(k_ref, ks_ref, kbases[g], w0s[g], kt),
                                   NT_DIMS, preferred_element_type=F32) for g in range(G)]
            ps, mns, lns = [], [], []
            for g in range(G):
                sel = (js[g] * qt - w0s[g]) // span
                sc = scs[g] + bias_ref[pl.ds(pl.multiple_of(sel * qt, qt), qt), :]
                mx = jnp.max(sc, axis=-1, keepdims=True)
                p = jnp.exp(sc - mx)
                lns.append(jnp.broadcast_to(jnp.sum(p, axis=-1, keepdims=True), (qt, HEAD)))
                mns.append(jnp.broadcast_to(mx, (qt, HEAD)))
                ps.append(p.astype(BF16))
            ons = [jnp.dot(ps[g], window(v_ref, vs_ref, kbases[g], w0s[g], kt),
                           preferred_element_type=F32) for g in range(G)]
            for g in range(G):
                mv = m_ref.at[pl.ds(qbases[g], qt * dil - slack)]
                lv = l_ref.at[pl.ds(qbases[g], qt * dil - slack)]
                av = acc_ref.at[pl.ds(qbases[g], qt * dil - slack)]
                if first:
                    mv[rows, :] = mns[g]
                    lv[rows, :] = lns[g]
                    av[rows, :] = ons[g]
                else:
                    mo = mv[rows, :]
                    mm = jnp.maximum(mo, mns[g])
                    wo = jnp.exp(mo - mm)
                    wn = jnp.exp(mns[g] - mm)
                    mv[rows, :] = mm
                    lv[rows, :] = wo * lv[rows, :] + wn * lns[g]
                    av[rows, :] = wo * av[rows, :] + wn * ons[g]
            return carry

        lax.fori_loop(0, ndyn * ngrp, tiles, 0)


def _attn_kernel(q_ref, k_ref, v_ref, bias_ref, o_ref, qs_ref, ks_ref, vs_ref, tmp_ref,
                 acc_ref, m_ref, l_ref):
    g = pl.program_id(2)
    for gi, (win, dil) in enumerate(ATTN_GROUPS):
        @pl.when(g == gi)
        def _(gi=gi, win=win, dil=dil):
            _attn_group(q_ref, k_ref, v_ref, bias_ref, qs_ref, ks_ref, vs_ref, tmp_ref,
                        acc_ref, m_ref, l_ref, dil=dil, span=(win // 2) // dil, first=(gi == 0))

    @pl.when(g == len(ATTN_GROUPS) - 1)
    def _():
        o_ref[0] = (acc_ref[...] * (1.0 / l_ref[...])).astype(o_ref.dtype)


def _band_bias(span):
    qt, kt = 2 * span, 4 * span
    row = np.arange(qt)[:, None]
    col = np.arange(kt)[None, :]
    blocks = [np.where(np.abs(col - k * span - row) <= span, 0.0, NEG) for k in range(3)]
    return jnp.asarray(np.concatenate(blocks, axis=0), dtype=F32)


def _dilated_attention(a_rot, a_plain):
    bsz, s, _ = a_rot.shape
    ng = len(ATTN_GROUPS)
    spans = {(win // 2) // dil for win, dil in ATTN_GROUPS}
    assert len(spans) == 1, "one band-mask table serves every group"
    bias = _band_bias(spans.pop())
    vbase = HG_HEADS
    blk = (1, s, HEAD)
    return pl.pallas_call(
        _attn_kernel,
        grid=(bsz, ATTN_HEADS, ng),
        in_specs=[pl.BlockSpec(blk, lambda b, h, g: (b, 0, 2 * g * ATTN_HEADS + h)),
                  pl.BlockSpec(blk, lambda b, h, g: (b, 0, (2 * g + 1) * ATTN_HEADS + h)),
                  pl.BlockSpec(blk, lambda b, h, g: (b, 0, vbase + g * ATTN_HEADS + h)),
                  pl.BlockSpec(bias.shape, lambda b, h, g: (0, 0))],
        out_specs=pl.BlockSpec(blk, lambda b, h, g: (b, 0, h)),
        out_shape=jax.ShapeDtypeStruct((bsz, s, ATTN_WIDTH), BF16),
        scratch_shapes=([pltpu.VMEM((s, HEAD), F32)] * 3 + [pltpu.VMEM((min(s, ATTN_SEGMENT), HEAD), F32)]
                        + [pltpu.VMEM((s, HEAD), F32)] * 3),
        compiler_params=_cparams(("parallel", "parallel", "arbitrary")),
        name="dilated_attn",
    )(a_rot, a_rot, a_plain, bias)


def _merge_xattn_kernel(*refs, bounds):
    nsrc = len(bounds) - 1
    (hg_ref, attn_ref, gate_ref, who_ref, wao_ref, wout_ref,
     g_ref, wq_ref, kv_ref, wo_ref, out_ref) = refs[nsrc:]
    x = _pick_source(pl.program_id(0), refs[:nsrc], bounds)
    d = x.shape[1]
    hd = d // XA_HEADS
    yh = jnp.dot(hg_ref[...], who_ref[...], preferred_element_type=F32)
    ya = jnp.dot(attn_ref[...], wao_ref[...], preferred_element_type=F32)
    merged = gate_ref[:, :d].astype(F32) * yh + gate_ref[:, d:].astype(F32) * ya
    hx = x + jnp.dot(merged.astype(BF16), wout_ref[...], preferred_element_type=F32)
    u = (hx * _rms_scale(hx) * g_ref[...]).astype(BF16)
    q = (jnp.dot(u, wq_ref[...], preferred_element_type=F32) * (hd ** -0.5)).astype(BF16)
    outs = []
    for h in range(XA_HEADS):
        kh = kv_ref[0, :, h * hd:(h + 1) * hd]
        vh = kv_ref[0, :, d + h * hd:d + (h + 1) * hd]
        s = lax.dot_general(q[:, h * hd:(h + 1) * hd], kh, NT_DIMS, preferred_element_type=F32)
        p = jnp.exp(s - jnp.max(s, axis=-1, keepdims=True))
        den = jnp.sum(p, axis=-1, keepdims=True)
        outs.append(jnp.dot(p.astype(BF16), vh, preferred_element_type=F32) * (1.0 / den))
    o = jnp.concatenate(outs, axis=1).astype(BF16)
    out_ref[...] = hx + jnp.dot(o, wo_ref[...], preferred_element_type=F32)


def _merge_xattn(srcs, hg, attn, gates, who, wao, wout, g, wq, kv, wo, seq_len, tm=512):
    d = srcs[0].shape[1]
    xspecs, bounds = _row_sources(srcs, tm)
    tiles_per_seq = seq_len // tm
    row = lambda w: pl.BlockSpec((tm, w), lambda i: (i, 0))
    full = lambda a: pl.BlockSpec(a.shape, lambda i: (0, 0))
    return pl.pallas_call(
        functools.partial(_merge_xattn_kernel, bounds=bounds),
        grid=(bounds[-1],),
        in_specs=xspecs + [row(d), row(ATTN_WIDTH), row(2 * d), full(who), full(wao), full(wout),
                           pl.BlockSpec((1, d), lambda i: (0, 0)), full(wq),
                           pl.BlockSpec((1,) + kv.shape[1:], lambda i: (i // tiles_per_seq, 0, 0)), full(wo)],
        out_specs=row(d),
        out_shape=jax.ShapeDtypeStruct((bounds[-1] * tm, d), F32),
        compiler_params=_cparams(("parallel",)),
        name="merge_xattn",
    )(*srcs, hg, attn, gates, who, wao, wout, g.reshape(1, d), wq, kv, wo)


def _mlp_kernel(h_ref, g_ref, w1_ref, w2_ref, gf_ref, *out_refs, bounds):
    hx = h_ref[...]
    u = (hx * _rms_scale(hx) * g_ref[...]).astype(BF16)
    a = jnp.maximum(jnp.dot(u, w1_ref[...], preferred_element_type=F32), 0.0)
    a = (a * a).astype(BF16)
    y = hx + jnp.dot(a, w2_ref[...], preferred_element_type=F32)
    y = y * _rms_scale(y) * gf_ref[...]
    i = pl.program_id(0)
    for k, out_ref in enumerate(out_refs):
        @pl.when(jnp.logical_and(i >= bounds[k], i < bounds[k + 1]))
        def _(out_ref=out_ref):
            out_ref[...] = y


def _mlp(h2d, g, w1, w2, gf, group_rows, tm=512):
    m, d = h2d.shape
    assert sum(group_rows) == m
    bounds = [0]
    for r in group_rows:
        assert r % tm == 0
        bounds.append(bounds[-1] + r // tm)
    full = lambda a: pl.BlockSpec(a.shape, lambda i: (0, 0))
    out_specs = [pl.BlockSpec((tm, d), lambda i, lo=bounds[k], n=bounds[k + 1] - bounds[k]:
                              (jnp.clip(i - lo, 0, n - 1), 0)) for k in range(len(group_rows))]
    return pl.pallas_call(
        functools.partial(_mlp_kernel, bounds=bounds),
        grid=(m // tm,),
        in_specs=[pl.BlockSpec((tm, d), lambda i: (i, 0)), pl.BlockSpec((1, d), lambda i: (0, 0)),
                  full(w1), full(w2), pl.BlockSpec((1, d), lambda i: (0, 0))],
        out_specs=out_specs,
        out_shape=[jax.ShapeDtypeStruct((r, d), F32) for r in group_rows],
        compiler_params=_cparams(("arbitrary",)),
        name="mlp_final",
    )(h2d, g.reshape(1, d), w1, w2, gf.reshape(1, d))


def _rotary_tables(s):
    half = ROT_DIM // 2
    inv = ROPE_THETA ** (-jnp.arange(half, dtype=F32) * 2.0 / ROT_DIM)
    ang = jnp.arange(s, dtype=F32)[:, None] * inv[None, :]
    cos, sin = jnp.cos(ang), jnp.sin(ang)
    pad = jnp.zeros((s, HEAD - ROT_DIM), F32)
    zero = jnp.zeros((s, half), F32)
    cos_t = jnp.concatenate([cos, cos, pad + 1.0], axis=1)
    sin_lo = jnp.concatenate([-sin, zero, pad], axis=1)
    sin_hi = jnp.concatenate([zero, sin, pad], axis=1)
    return cos_t, sin_lo, sin_hi


def _encode(xs, mems, mix_norm_g, w_in, hgrn_lb_logits, hgrn_gnorm_g, w_hgrn_o, w_attn_o, w_out,
            xa_norm_g, mem_norm_g, w_xq, w_xkv, w_xo, ffn_norm_g, w_ffn1, w_ffn2, final_norm_g,
            hgrn_rows=2048):
    s, d = xs[0].shape[1:]
    bsz = sum(x.shape[0] for x in xs)
    t = bsz * s
    depth = w_in.shape[0]
    assert depth == 1, "the final norm is fused into the (single) layer's MLP call"
    l = 0
    fd = HG_HEADS * HEAD
    lb_all = jnp.cumsum(jax.nn.softmax(hgrn_lb_logits.astype(F32), axis=1), axis=1)
    sizes = (fd,) * 5 + (ATTN_WIDTH,) * 9 + (d, d)
    offs = np.concatenate([[0], np.cumsum(sizes)])
    cos_t, sin_lo, sin_hi = _rotary_tables(s)
    tm = min(TOKEN_TILE, s)
    tmp = min(PROJ_TOKEN_TILE, s)
    tabspec = pl.BlockSpec((tmp, HEAD), lambda i: (i % (s // tmp), 0))
    srcs = [x.reshape(-1, d) for x in xs]
    mem = jnp.concatenate(mems, axis=0) if len(mems) > 1 else mems[0]

    wl = w_in[l]
    seg = lambda p: wl[:, offs[p]:offs[p + 1]]
    bf = lambda a: a.astype(BF16)
    w_silu = bf(jnp.concatenate([seg(0), seg(4)], axis=1))
    w_lf = bf(jnp.concatenate([seg(1), seg(2)], axis=1))
    w_plain = bf(jnp.concatenate([seg(3), seg(7), seg(10), seg(13)], axis=1))
    w_rot = bf(jnp.concatenate([seg(5), seg(6), seg(8), seg(9), seg(11), seg(12)], axis=1))
    w_gate = bf(jnp.concatenate([seg(14), seg(15)], axis=1))
    g_mix = mix_norm_g[l]

    a_silu, u = _first_proj(srcs, g_mix, w_silu, _ep_silu, BF16, tmp, "proj_silu")
    proj = functools.partial(_norm_proj, u, g_mix, tm=tmp, norm=False)
    lb_row = lb_all[:, l].reshape(1, 2 * fd)
    a_lf = proj(w_lf, _ep_logf, F32, extras=(lb_row,),
                extra_specs=(pl.BlockSpec((1, 2 * fd), lambda i: (0, 0)),), name="proj_logf")
    a_plain = proj(w_plain, _ep_plain, BF16, name="proj_plain")
    qscale = jnp.tile(jnp.concatenate([jnp.full((ATTN_WIDTH,), HEAD ** -0.5, F32),
                                       jnp.ones((ATTN_WIDTH,), F32)]), 3).reshape(1, 6 * ATTN_WIDTH)
    a_rot = proj(w_rot, _ep_rotary, BF16, extras=(cos_t, sin_lo, sin_hi, qscale),
                 extra_specs=(tabspec, tabspec, tabspec, pl.BlockSpec((1, 6 * ATTN_WIDTH), lambda i: (0, 0))),
                 name="proj_rotary")
    a_gate = proj(w_gate, _ep_sigmoid, BF16, name="proj_gate")

    a_silu3 = a_silu.reshape(bsz, s, 2 * fd)
    a_lf3 = a_lf.reshape(bsz, s, 2 * fd)
    a_plain3 = a_plain.reshape(bsz, s, -1)
    o_fwd = _hgrn_pass(a_silu3, a_lf3, a_plain3, hgrn_gnorm_g[l], None, False, hgrn_rows)
    hg = _hgrn_pass(a_silu3, a_lf3, a_plain3, hgrn_gnorm_g[l], o_fwd, True, hgrn_rows)

    attn = _dilated_attention(a_rot.reshape(bsz, s, -1), a_plain3)

    nm = mem.shape[1]
    kv = _norm_proj(mem.reshape(bsz * nm, d), mem_norm_g[l], w_xkv[l].astype(BF16), _ep_plain, BF16,
                    tm=nm, name="proj_memkv")
    h2d = _merge_xattn(srcs, hg.reshape(t, fd), attn.reshape(t, ATTN_WIDTH), a_gate,
                       w_hgrn_o[l].astype(BF16), w_attn_o[l].astype(BF16), w_out[l].astype(BF16),
                       xa_norm_g[l], w_xq[l].astype(BF16), kv.reshape(bsz, nm, 2 * d), w_xo[l].astype(BF16),
                       s, tm=tm)
    outs = _mlp(h2d, ffn_norm_g[l], w_ffn1[l].astype(BF16), w_ffn2[l].astype(BF16), final_norm_g,
                [x.shape[0] * s for x in xs], tm=tm)
    return [o.reshape(x.shape) for o, x in zip(outs, xs)]


def kernel(x_prompt, x_sample, mem_prompt, mem_sample, mix_norm_g, w_in, hgrn_lb_logits, hgrn_gnorm_g,
           w_hgrn_o, w_attn_o, w_out, xa_norm_g, mem_norm_g, w_xq, w_xkv, w_xo, ffn_norm_g, w_ffn1,
           w_ffn2, final_norm_g):
    assert x_prompt.shape[1:] == x_sample.shape[1:]
    y_prompt, y_sample = _encode(
        [x_prompt, x_sample], [mem_prompt, mem_sample], mix_norm_g, w_in, hgrn_lb_logits, hgrn_gnorm_g,
        w_hgrn_o, w_attn_o, w_out, xa_norm_g, mem_norm_g, w_xq, w_xkv, w_xo, ffn_norm_g, w_ffn1, w_ffn2,
        final_norm_g)
    return y_prompt, y_sample
```

```python
import functools

import numpy as np
import jax
import jax.numpy as jnp
from jax import lax
from jax.experimental import pallas as pl
from jax.experimental.pallas import tpu as pltpu

F32 = jnp.float32
BF16 = jnp.bfloat16

RMS_EPS = 1e-6
ROPE_THETA = 500000.0
HG_HEADS = 8
HEAD = 128
ATTN_HEADS = 4
ATTN_WIDTH = ATTN_HEADS * HEAD
ATTN_GROUPS = ((128, 1), (512, 4), (2048, 16))
ROT_DIM = HEAD // 4
XA_HEADS = 4
CHUNK = 64
SUB = 8
NEG = -1e30
HG_PACK = 2
SAFE_CHUNK_LOG_DECAY = -60.0
HGRN_UNROLL = 16
ATTN_UNROLL = 4
ATTN_SEGMENT = 1024
NT_DIMS = (((1,), (1,)), ((), ()))
TN_DIMS = (((0,), (0,)), ((), ()))

TOKEN_TILE = 512
PROJ_TOKEN_TILE = 1024

VMEM_LIMIT = 56 * 1024 * 1024


def _cparams(sem):
    return pltpu.CompilerParams(dimension_semantics=sem, vmem_limit_bytes=VMEM_LIMIT)


def _sigmoid(x):
    return 1.0 / (1.0 + jnp.exp(-x))


def _rms_scale(xf):
    return lax.rsqrt(jnp.mean(xf * xf, axis=-1, keepdims=True) + RMS_EPS)


def _ep_plain(acc):
    return acc


def _sigmoid_tanh(x):
    return 0.5 * jnp.tanh(0.5 * x) + 0.5


def _ep_silu(acc):
    return acc * _sigmoid_tanh(acc)


def _ep_sigmoid(acc):
    return _sigmoid_tanh(acc)


def _ep_logf(acc, lb):
    return jnp.log(lb + (1.0 - lb) * _sigmoid(acc))


def _ep_rotary(acc, cos, sin_lo, sin_hi, colscale):
    half = ROT_DIM // 2
    outs = []
    for c in range(acc.shape[1] // HEAD):
        t = acc[:, c * HEAD:(c + 1) * HEAD]
        r = t * cos + pltpu.roll(t, HEAD - half, 1) * sin_lo + pltpu.roll(t, half, 1) * sin_hi
        outs.append(r)
    return jnp.concatenate(outs, axis=1) * colscale


def _row_sources(srcs, tm):
    bounds = [0]
    for a in srcs:
        assert a.shape[0] % tm == 0
        bounds.append(bounds[-1] + a.shape[0] // tm)
    specs = [pl.BlockSpec((tm, a.shape[1]),
                          lambda i, lo=bounds[k], n=bounds[k + 1] - bounds[k]: (jnp.clip(i - lo, 0, n - 1), 0))
             for k, a in enumerate(srcs)]
    return specs, bounds


def _pick_source(i, refs, bounds):
    x = refs[-1][...]
    for k in range(len(refs) - 2, -1, -1):
        x = jnp.where(i < bounds[k + 1], refs[k][...], x)
    return x


def _first_proj_kernel(*refs, epilogue, bounds):
    nsrc = len(bounds) - 1
    g_ref, w_ref, o_ref, u_ref = refs[nsrc:]
    xf = _pick_source(pl.program_id(0), refs[:nsrc], bounds)
    u = (xf * _rms_scale(xf) * g_ref[...]).astype(BF16)
    u_ref[...] = u
    o_ref[...] = epilogue(jnp.dot(u, w_ref[...], preferred_element_type=F32)).astype(o_ref.dtype)


def _first_proj(srcs, g, w, epilogue, out_dtype, tm, name):
    d = srcs[0].shape[1]
    n = w.shape[1]
    specs, bounds = _row_sources(srcs, tm)
    m = bounds[-1] * tm
    return pl.pallas_call(
        functools.partial(_first_proj_kernel, epilogue=epilogue, bounds=bounds),
        grid=(bounds[-1],),
        in_specs=specs + [pl.BlockSpec((1, d), lambda i: (0, 0)), pl.BlockSpec((d, n), lambda i: (0, 0))],
        out_specs=[pl.BlockSpec((tm, n), lambda i: (i, 0)), pl.BlockSpec((tm, d), lambda i: (i, 0))],
        out_shape=[jax.ShapeDtypeStruct((m, n), out_dtype), jax.ShapeDtypeStruct((m, d), BF16)],
        compiler_params=_cparams(("parallel",)),
        name=name,
    )(*srcs, g.reshape(1, d), w)


def _norm_proj_kernel(*refs, epilogue, n_extra, norm):
    x_ref, g_ref, w_ref = refs[:3]
    extra = refs[3:3 + n_extra]
    o_ref = refs[3 + n_extra]
    if norm:
        xf = x_ref[...]
        u = (xf * _rms_scale(xf) * g_ref[...]).astype(BF16)
    else:
        u = x_ref[...]
    acc = jnp.dot(u, w_ref[...], preferred_element_type=F32)
    o_ref[...] = epilogue(acc, *[e[...] for e in extra]).astype(o_ref.dtype)


def _norm_proj(x2d, g, w, epilogue, out_dtype, extras=(), extra_specs=(), tm=512, name="norm_proj", norm=True):
    m, d = x2d.shape
    n = w.shape[1]
    tm = min(tm, m)
    assert m % tm == 0
    in_specs = [
        pl.BlockSpec((tm, d), lambda i: (i, 0)),
        pl.BlockSpec((1, d), lambda i: (0, 0)),
        pl.BlockSpec((d, n), lambda i: (0, 0)),
    ] + list(extra_specs)
    return pl.pallas_call(
        functools.partial(_norm_proj_kernel, epilogue=epilogue, n_extra=len(extras), norm=norm),
        grid=(m // tm,),
        in_specs=in_specs,
        out_specs=pl.BlockSpec((tm, n), lambda i: (i, 0)),
        out_shape=jax.ShapeDtypeStruct((m, n), out_dtype),
        compiler_params=_cparams(("parallel",)),
        name=name,
    )(x2d, g.reshape(1, d), w, *extras)


def _hgrn_robust_chunk(r0, ls, q_ref, lf_ref, v_ref, tri_ref, st_ref, kc_ref, bc_ref, rev):
    C = CHUNK
    ng = C // SUB
    rowl = lax.broadcasted_iota(jnp.int32, (C, HEAD), 0)
    sub_pos = rowl & (SUB - 1)
    arow = lax.broadcasted_iota(jnp.int32, (C, C), 0)
    acol = lax.broadcasted_iota(jnp.int32, (C, C), 1)
    ones_b = jnp.ones((HEAD, HEAD), BF16)
    zeros_g = jnp.zeros((SUB, HEAD), F32)

    lf = lf_ref[0, pl.ds(r0, C), ls]
    q = q_ref[0, pl.ds(r0, C), ls].astype(F32)
    v = v_ref[0, pl.ds(r0, C), ls]
    k = 1.0 - jnp.exp(lf)
    hi = lf.astype(BF16)
    mid = (lf - hi.astype(F32)).astype(BF16)
    b = jnp.dot(tri_ref[...], jnp.concatenate([hi, mid], axis=0), preferred_element_type=F32)
    kc_ref[pl.ds(SUB, C), :] = k
    bc_ref[pl.ds(SUB, C), :] = b
    b_end = bc_ref[pl.ds(SUB + (0 if rev else C - 1), 1), :]

    st = st_ref[:, ls]
    qi = (q * jnp.exp(b)).astype(BF16)
    o = lax.dot_general(qi, st.astype(BF16), NT_DIMS, preferred_element_type=F32)
    kl = (k * jnp.exp(b_end - b)).astype(BF16)
    st_ref[:, ls] = st * jnp.exp(b_end) + lax.dot_general(v, kl, TN_DIMS, preferred_element_type=F32)

    a = jnp.zeros((C, C), F32)
    h = C // 2
    while h >= SUB:
        qparts, kparts = [], []
        for gi in range(ng):
            t0 = gi * SUB
            blk = t0 // (2 * h)
            in_upper = (t0 % (2 * h)) >= h
            is_query = (not in_upper) if rev else in_upper
            rr = blk * 2 * h + (h if rev else h - 1)
            bref = bc_ref[pl.ds(SUB + rr, 1), :]
            bg = b[t0:t0 + SUB]
            if is_query:
                qparts.append(q[t0:t0 + SUB] * jnp.exp(bg - bref))
                kparts.append(zeros_g)
            else:
                qparts.append(zeros_g)
                kparts.append(k[t0:t0 + SUB] * jnp.exp(bref - bg))
        qh = jnp.concatenate(qparts, axis=0).astype(BF16)
        kh = jnp.concatenate(kparts, axis=0).astype(BF16)
        ah = lax.dot_general(qh, kh, NT_DIMS, preferred_element_type=F32)
        if 2 * h < C:
            ah = jnp.where((arow ^ acol) < 2 * h, ah, 0.0)
        a = a + ah
        h //= 2

    for d in range(SUB):
        sh = SUB + (d if rev else -d)
        ks = kc_ref[pl.ds(sh, C), :]
        bs = bc_ref[pl.ds(sh, C), :]
        ok = (sub_pos + d <= SUB - 1) if rev else (sub_pos >= d)
        p = jnp.where(ok, q * ks * jnp.exp(b - bs), 0.0).astype(BF16)
        rs = jnp.dot(p, ones_b, preferred_element_type=F32)[:, :C]
        tgt = (arow + d) if rev else (arow - d)
        a = a + jnp.where(acol == tgt, rs, 0.0)

    return o + jnp.dot(a.astype(BF16), v, preferred_element_type=F32)


def _blockdiag2(x):
    z = jnp.zeros((x.shape[0], HEAD), x.dtype)
    return jnp.concatenate([jnp.concatenate([x[:, :HEAD], z], axis=1),
                            jnp.concatenate([z, x[:, HEAD:]], axis=1)], axis=0)


def _hgrn_kernel(*refs, rev, final, nchunk):
    if final:
        (q_ref, lf_ref, v_ref, tri_ref, of_ref, g_ref, gn_ref, o_ref,
         st_ref, qi_ref, a_ref, upd_ref, snap_ref, dec_ref, b_ref, kc_ref, bc_ref) = refs
    else:
        (q_ref, lf_ref, v_ref, tri_ref, o_ref,
         st_ref, qi_ref, a_ref, upd_ref, snap_ref, dec_ref, b_ref, kc_ref, bc_ref) = refs
        of_ref = g_ref = gn_ref = None
    C = CHUNK
    W = HG_PACK * HEAD

    @pl.when(pl.program_id(2) == 0)
    def _():
        st_ref[...] = jnp.zeros_like(st_ref)
        kc_ref[...] = jnp.zeros_like(kc_ref)
        bc_ref[...] = jnp.zeros_like(bc_ref)

    def emit(o, r0, ls):
        if final:
            tot = o + of_ref[0, pl.ds(r0, C), ls]
            parts = []
            for j in range(tot.shape[1] // HEAD):
                tj = tot[:, j * HEAD:(j + 1) * HEAD]
                parts.append(tj * _rms_scale(tj) * gn_ref[...])
            y = parts[0] if len(parts) == 1 else jnp.concatenate(parts, axis=1)
            y = y * g_ref[0, pl.ds(r0, C), ls].astype(F32)
            o_ref[0, pl.ds(r0, C), ls] = y.astype(o_ref.dtype)
        else:
            o_ref[0, pl.ds(r0, C), ls] = o

    G = HGRN_UNROLL
    G2 = 2 * G
    assert nchunk % G2 == 0

    def phase0(i, carry):
        r0s = [pl.multiple_of((i * G2 + g) * C, C) for g in range(G2)]
        lfs = [lf_ref[0, pl.ds(r0, C), :] for r0 in r0s]
        his = [lf.astype(BF16) for lf in lfs]
        mids = [(lf - hi.astype(F32)).astype(BF16) for lf, hi in zip(lfs, his)]
        rhs = jnp.concatenate([jnp.concatenate(his, axis=1), jnp.concatenate(mids, axis=1)], axis=0)
        ball = jnp.dot(tri_ref[...], rhs, preferred_element_type=F32)
        for g in range(G2):
            b = ball[:, g * W:(g + 1) * W]
            b_ref[pl.ds(r0s[g], C), :] = b
            dec_ref[pl.ds(pl.multiple_of((i * G2 + g) * SUB, SUB), SUB), :] = jnp.broadcast_to(
                b[0:1] if rev else b[C - 1:C], (SUB, W))
        return carry

    lax.fori_loop(0, nchunk // G2, phase0, 0)

    safe = jnp.min(dec_ref[...]) >= SAFE_CHUNK_LOG_DECAY

    @pl.when(safe)
    def _():
        arow = lax.broadcasted_iota(jnp.int32, (C, 2 * C), 0)
        acol = lax.broadcasted_iota(jnp.int32, (C, 2 * C), 1) & (C - 1)
        causal = (acol >= arow) if rev else (acol <= arow)

        def phase1(i, carry):
            cs = [i * G + g for g in range(G)]
            r0s = [pl.multiple_of(c * C, C) for c in cs]
            lfs = [lf_ref[0, pl.ds(r0, C), :] for r0 in r0s]
            qis, kls, kbars = [], [], []
            for g in range(G):
                b = b_ref[pl.ds(r0s[g], C), :]
                k = 1.0 - jnp.exp(lfs[g])
                q = q_ref[0, pl.ds(r0s[g], C), :].astype(F32)
                qi = (q * jnp.exp(b)).astype(BF16)
                qi_ref[pl.ds(r0s[g], C), :] = qi
                dec = jnp.exp(b[0:1] if rev else b[C - 1:C])
                dec_ref[pl.ds(pl.multiple_of(cs[g] * SUB, SUB), SUB), :] = jnp.broadcast_to(dec, (SUB, W))
                qis.append(qi)
                kbar = k * jnp.exp(-b)
                kls.append((kbar * dec).astype(BF16))
                kbars.append(kbar.astype(BF16))
            avals = [lax.dot_general(qis[g], _blockdiag2(kbars[g]), NT_DIMS, preferred_element_type=F32)
                     for g in range(G)]
            for g in range(G):
                v = v_ref[0, pl.ds(r0s[g], C), :]
                for j in range(HG_PACK):
                    ls = slice(j * HEAD, (j + 1) * HEAD)
                    upd_ref[pl.ds(pl.multiple_of(cs[g] * HEAD, HEAD), HEAD), ls] = lax.dot_general(
                        v[:, ls], kls[g][:, ls], TN_DIMS, preferred_element_type=F32)
            for g in range(G):
                a_ref[pl.ds(r0s[g], C), :] = jnp.where(causal, avals[g], 0.0).astype(BF16)
            return carry

        lax.fori_loop(0, nchunk // G, phase1, 0)

        def phase2(i, carry):
            c = (nchunk - 1 - i) if rev else i
            s0 = pl.multiple_of(c * HEAD, HEAD)
            st = st_ref[...]
            snap_ref[pl.ds(s0, HEAD), :] = st.astype(BF16)
            dec = dec_ref[pl.ds(pl.multiple_of(c * SUB, SUB), 1), :]
            st_ref[...] = st * dec + upd_ref[pl.ds(s0, HEAD), :]
            return carry

        lax.fori_loop(0, nchunk, phase2, 0, unroll=G)

        def phase3(i, carry):
            cs = [i * G + g for g in range(G)]
            r0s = [pl.multiple_of(c * C, C) for c in cs]
            snaps = [snap_ref[pl.ds(pl.multiple_of(c * HEAD, HEAD), HEAD), :] for c in cs]
            o1 = [lax.dot_general(qi_ref[pl.ds(r0s[g], C), :], _blockdiag2(snaps[g]), NT_DIMS,
                                  preferred_element_type=F32) for g in range(G)]
            o2 = [jnp.dot(a_ref[pl.ds(r0s[g], C), :], _blockdiag2(v_ref[0, pl.ds(r0s[g], C), :]),
                          preferred_element_type=F32) for g in range(G)]
            for g in range(G):
                emit(o1[g] + o2[g], r0s[g], slice(None))
            return carry

        lax.fori_loop(0, nchunk // G, phase3, 0)

    @pl.when(jnp.logical_not(safe))
    def _():
        def body(i, carry):
            c = (nchunk - 1 - i) if rev else i
            r0 = pl.multiple_of(c * C, C)
            for j in range(HG_PACK):
                ls = slice(j * HEAD, (j + 1) * HEAD)
                o = _hgrn_robust_chunk(r0, ls, q_ref, lf_ref, v_ref, tri_ref, st_ref, kc_ref, bc_ref, rev)
                emit(o, r0, ls)
            return carry

        lax.fori_loop(0, nchunk, body, 0)


def _tri_matrix(rev):
    t = np.arange(CHUNK)
    m = (t[None, :] >= t[:, None]) if rev else (t[None, :] <= t[:, None])
    return jnp.asarray(np.concatenate([m, m], axis=1).astype(np.float32), dtype=BF16)


def _hgrn_pass(a_silu, a_lf, a_plain, gnorm, o_fwd, rev, rows):
    bsz, s, _ = a_silu.shape
    rows = min(rows, s)
    nblk = s // rows
    nchunk = rows // CHUNK
    w = HG_PACK * HEAD
    npk = HG_HEADS // HG_PACK
    final = o_fwd is not None
    seq = (lambda i: nblk - 1 - i) if rev else (lambda i: i)
    fcol = npk if rev else 0
    in_specs = [
        pl.BlockSpec((1, rows, w), lambda b, h, i: (b, seq(i), h)),
        pl.BlockSpec((1, rows, w), lambda b, h, i: (b, seq(i), h + fcol)),
        pl.BlockSpec((1, rows, w), lambda b, h, i: (b, seq(i), h)),
        pl.BlockSpec((CHUNK, 2 * CHUNK), lambda b, h, i: (0, 0)),
    ]
    args = [a_silu, a_lf, a_plain, _tri_matrix(rev)]
    if final:
        in_specs += [
            pl.BlockSpec((1, rows, w), lambda b, h, i: (b, seq(i), h)),
            pl.BlockSpec((1, rows, w), lambda b, h, i: (b, seq(i), h + npk)),
            pl.BlockSpec((1, HEAD), lambda b, h, i: (0, 0)),
        ]
        args += [o_fwd, a_silu, gnorm.reshape(1, HEAD)]
    return pl.pallas_call(
        functools.partial(_hgrn_kernel, rev=rev, final=final, nchunk=nchunk),
        grid=(bsz, npk, nblk),
        in_specs=in_specs,
        out_specs=pl.BlockSpec((1, rows, w), lambda b, h, i: (b, seq(i), h)),
        out_shape=jax.ShapeDtypeStruct((bsz, s, HG_HEADS * HEAD), BF16 if final else F32),
        scratch_shapes=[
            pltpu.VMEM((HEAD, w), F32),
            pltpu.VMEM((rows, w), BF16),
            pltpu.VMEM((rows, 2 * CHUNK), BF16),
            pltpu.VMEM((nchunk * HEAD, w), F32),
            pltpu.VMEM((nchunk * HEAD, w), BF16),
            pltpu.VMEM((nchunk * SUB, w), F32),
            pltpu.VMEM((rows, w), F32),
            pltpu.VMEM((CHUNK + 2 * SUB, HEAD), F32),
            pltpu.VMEM((CHUNK + 2 * SUB, HEAD), F32),
        ],
        compiler_params=_cparams(("parallel", "parallel", "arbitrary")),
        name="hgrn_bwd" if rev else "hgrn_fwd",
    )(*args)


def _attn_group(q_ref, k_ref, v_ref, bias_ref, qs_ref, ks_ref, vs_ref, tmp_ref,
                acc_ref, m_ref, l_ref, *, dil, span, first):
    s = q_ref.shape[1]
    length = s // dil
    qt, kt = 2 * span, 4 * span
    ntile = length // qt
    G = min(ATTN_UNROLL if dil > 1 else 2 * ATTN_UNROLL, ntile)
    assert ntile % G == 0
    ngrp = ntile // G
    nstatic = min(dil, SUB)
    ndyn = dil // nstatic
    slack = SUB if ndyn > 1 else 0
    resmajor = ndyn > 1
    if resmajor:
        nseg = tmp_ref.shape[0] // dil
        for src, dst in ((q_ref, qs_ref), (k_ref, ks_ref), (v_ref, vs_ref)):
            def segment(c, carry, src=src, dst=dst):
                n0 = pl.multiple_of(c * (nseg * dil), nseg * dil)
                tmp_ref[...] = src[0, pl.ds(n0, nseg * dil), :].astype(F32)
                for r in range(dil):
                    dst[pl.ds(pl.multiple_of(r * length + c * nseg, nseg), nseg), :] = (
                        tmp_ref[pl.ds(r, nseg, stride=dil), :])
                return carry

            lax.fori_loop(0, length // nseg, segment, 0)
    elif dil > 1:
        qs_ref[...] = q_ref[0].astype(F32)
        ks_ref[...] = k_ref[0].astype(F32)
        vs_ref[...] = v_ref[0].astype(F32)

    for rs in range(nstatic):
        def tiles(i, carry):
            js = [(i % ngrp) * G + g for g in range(G)]
            w0s = [jnp.clip(j * qt - span, 0, length - kt) for j in js]
            if ndyn > 1:
                rd = i // ngrp
                qbases = [pl.multiple_of(j * (qt * dil) + rd * SUB, SUB) for j in js]
                kbases = [pl.multiple_of(w0 * dil + rd * SUB, SUB) for w0 in w0s]
                res0 = (rs + rd * SUB) * length
            else:
                qbases = [pl.multiple_of(j * (qt * dil), qt * dil) for j in js]
                kbases = [pl.multiple_of(w0 * dil, span * dil) for w0 in w0s]
            rows = pl.ds(rs, qt, stride=dil) if dil > 1 else slice(None)

            def window(ref, sref, base, first_row, n):
                if resmajor:
                    return sref[pl.ds(pl.multiple_of(res0 + first_row, span), n), :].astype(BF16)
                if dil > 1:
                    return sref.at[pl.ds(base, n * dil - slack)][pl.ds(rs, n, stride=dil), :].astype(BF16)
                return ref[0, pl.ds(base, n), :]

            scs = [lax.dot_general(window(q_ref, qs_ref, qbases[g], js[g] * qt, qt),
                                   window(k_ref, ks_ref, kbases[g], w0s[g], kt),
                                   NT_DIMS, preferred_element_type=F32) for g in range(G)]
            ps, mns, lns = [], [], []
            for g in range(G):
                sel = (js[g] * qt - w0s[g]) // span
                sc = scs[g] + bias_ref[pl.ds(pl.multiple_of(sel * qt, qt), qt), :]
                mx = jnp.max(sc, axis=-1, keepdims=True)
                p = jnp.exp(sc - mx)
                lns.append(jnp.broadcast_to(jnp.sum(p, axis=-1, keepdims=True), (qt, HEAD)))
                mns.append(jnp.broadcast_to(mx, (qt, HEAD)))
                ps.append(p.astype(BF16))
            ons = [jnp.dot(ps[g], window(v_ref, vs_ref, kbases[g], w0s[g], kt),
                           preferred_element_type=F32) for g in range(G)]
            for g in range(G):
                mv = m_ref.at[pl.ds(qbases[g], qt * dil - slack)]
                lv = l_ref.at[pl.ds(qbases[g], qt * dil - slack)]
                av = acc_ref.at[pl.ds(qbases[g], qt * dil - slack)]
                if first:
                    mv[rows, :] = mns[g]
                    lv[rows, :] = lns[g]
                    av[rows, :] = ons[g]
                else:
                    mo = mv[rows, :]
                    mm = jnp.maximum(mo, mns[g])
                    wo = jnp.exp(mo - mm)
                    wn = jnp.exp(mns[g] - mm)
                    mv[rows, :] = mm
                    lv[rows, :] = wo * lv[rows, :] + wn * lns[g]
                    av[rows, :] = wo * av[rows, :] + wn * ons[g]
            return carry

        lax.fori_loop(0, ndyn * ngrp, tiles, 0)


def _attn_kernel(q_ref, k_ref, v_ref, bias_ref, o_ref, qs_ref, ks_ref, vs_ref, tmp_ref,
                 acc_ref, m_ref, l_ref):
    g = pl.program_id(2)
    for gi, (win, dil) in enumerate(ATTN_GROUPS):
        @pl.when(g == gi)
        def _(gi=gi, win=win, dil=dil):
            _attn_group(q_ref, k_ref, v_ref, bias_ref, qs_ref, ks_ref, vs_ref, tmp_ref,
                        acc_ref, m_ref, l_ref, dil=dil, span=(win // 2) // dil, first=(gi == 0))

    @pl.when(g == len(ATTN_GROUPS) - 1)
    def _():
        o_ref[0] = (acc_ref[...] * (1.0 / l_ref[...])).astype(o_ref.dtype)


def _band_bias(span):
    qt, kt = 2 * span, 4 * span
    row = np.arange(qt)[:, None]
    col = np.arange(kt)[None, :]
    blocks = [np.where(np.abs(col - k * span - row) <= span, 0.0, NEG) for k in range(3)]
    return jnp.asarray(np.concatenate(blocks, axis=0), dtype=F32)


def _dilated_attention(a_rot, a_plain):
    bsz, s, _ = a_rot.shape
    ng = len(ATTN_GROUPS)
    spans = {(win // 2) // dil for win, dil in ATTN_GROUPS}
    assert len(spans) == 1, "one band-mask table serves every group"
    bias = _band_bias(spans.pop())
    vbase = HG_HEADS
    blk = (1, s, HEAD)
    return pl.pallas_call(
        _attn_kernel,
        grid=(bsz, ATTN_HEADS, ng),
        in_specs=[pl.BlockSpec(blk, lambda b, h, g: (b, 0, 2 * g * ATTN_HEADS + h)),
                  pl.BlockSpec(blk, lambda b, h, g: (b, 0, (2 * g + 1) * ATTN_HEADS + h)),
                  pl.BlockSpec(blk, lambda b, h, g: (b, 0, vbase + g * ATTN_HEADS + h)),
                  pl.BlockSpec(bias.shape, lambda b, h, g: (0, 0))],
        out_specs=pl.BlockSpec(blk, lambda b, h, g: (b, 0, h)),
        out_shape=jax.ShapeDtypeStruct((bsz, s, ATTN_WIDTH), BF16),
        scratch_shapes=([pltpu.VMEM((s, HEAD), F32)] * 3 + [pltpu.VMEM((min(s, ATTN_SEGMENT), HEAD), F32)]
                        + [pltpu.VMEM((s, HEAD), F32)] * 3),
        compiler_params=_cparams(("parallel", "parallel", "arbitrary")),
        name="dilated_attn",
    )(a_rot, a_rot, a_plain, bias)


def _merge_xattn_kernel(*refs, bounds):
    nsrc = len(bounds) - 1
    (hg_ref, attn_ref, gate_ref, who_ref, wao_ref, wout_ref,
     g_ref, wq_ref, kv_ref, wo_ref, out_ref) = refs[nsrc:]
    x = _pick_source(pl.program_id(0), refs[:nsrc], bounds)
    d = x.shape[1]
    hd = d // XA_HEADS
    yh = jnp.dot(hg_ref[...], who_ref[...], preferred_element_type=F32)
    ya = jnp.dot(attn_ref[...], wao_ref[...], preferred_element_type=F32)
    merged = gate_ref[:, :d].astype(F32) * yh + gate_ref[:, d:].astype(F32) * ya
    hx = x + jnp.dot(merged.astype(BF16), wout_ref[...], preferred_element_type=F32)
    u = (hx * _rms_scale(hx) * g_ref[...]).astype(BF16)
    q = (jnp.dot(u, wq_ref[...], preferred_element_type=F32) * (hd ** -0.5)).astype(BF16)
    outs = []
    for h in range(XA_HEADS):
        kh = kv_ref[0, :, h * hd:(h + 1) * hd]
        vh = kv_ref[0, :, d + h * hd:d + (h + 1) * hd]
        s = lax.dot_general(q[:, h * hd:(h + 1) * hd], kh, NT_DIMS, preferred_element_type=F32)
        p = jnp.exp(s - jnp.max(s, axis=-1, keepdims=True))
        den = jnp.sum(p, axis=-1, keepdims=True)
        outs.append(jnp.dot(p.astype(BF16), vh, preferred_element_type=F32) * (1.0 / den))
    o = jnp.concatenate(outs, axis=1).astype(BF16)
    out_ref[...] = hx + jnp.dot(o, wo_ref[...], preferred_element_type=F32)


def _merge_xattn(srcs, hg, attn, gates, who, wao, wout, g, wq, kv, wo, seq_len, tm=512):
    d = srcs[0].shape[1]
    xspecs, bounds = _row_sources(srcs, tm)
    tiles_per_seq = seq_len // tm
    row = lambda w: pl.BlockSpec((tm, w), lambda i: (i, 0))
    full = lambda a: pl.BlockSpec(a.shape, lambda i: (0, 0))
    return pl.pallas_call(
        functools.partial(_merge_xattn_kernel, bounds=bounds),
        grid=(bounds[-1],),
        in_specs=xspecs + [row(d), row(ATTN_WIDTH), row(2 * d), full(who), full(wao), full(wout),
                           pl.BlockSpec((1, d), lambda i: (0, 0)), full(wq),
                           pl.BlockSpec((1,) + kv.shape[1:], lambda i: (i // tiles_per_seq, 0, 0)), full(wo)],
        out_specs=row(d),
        out_shape=jax.ShapeDtypeStruct((bounds[-1] * tm, d), F32),
        compiler_params=_cparams(("parallel",)),
        name="merge_xattn",
    )(*srcs, hg, attn, gates, who, wao, wout, g.reshape(1, d), wq, kv, wo)


def _mlp_kernel(h_ref, g_ref, w1_ref, w2_ref, gf_ref, *out_refs, bounds):
    hx = h_ref[...]
    u = (hx * _rms_scale(hx) * g_ref[...]).astype(BF16)
    a = jnp.maximum(jnp.dot(u, w1_ref[...], preferred_element_type=F32), 0.0)
    a = (a * a).astype(BF16)
    y = hx + jnp.dot(a, w2_ref[...], preferred_element_type=F32)
    y = y * _rms_scale(y) * gf_ref[...]
    i = pl.program_id(0)
    for k, out_ref in enumerate(out_refs):
        @pl.when(jnp.logical_and(i >= bounds[k], i < bounds[k + 1]))
        def _(out_ref=out_ref):
            out_ref[...] = y


def _mlp(h2d, g, w1, w2, gf, group_rows, tm=512):
    m, d = h2d.shape
    assert sum(group_rows) == m
    bounds = [0]
    for r in group_rows:
        assert r % tm == 0
        bounds.append(bounds[-1] + r // tm)
    full = lambda a: pl.BlockSpec(a.shape, lambda i: (0, 0))
    out_specs = [pl.BlockSpec((tm, d), lambda i, lo=bounds[k], n=bounds[k + 1] - bounds[k]:
                              (jnp.clip(i - lo, 0, n - 1), 0)) for k in range(len(group_rows))]
    return pl.pallas_call(
        functools.partial(_mlp_kernel, bounds=bounds),
        grid=(m // tm,),
        in_specs=[pl.BlockSpec((tm, d), lambda i: (i, 0)), pl.BlockSpec((1, d), lambda i: (0, 0)),
                  full(w1), full(w2), pl.BlockSpec((1, d), lambda i: (0, 0))],
        out_specs=out_specs,
        out_shape=[jax.ShapeDtypeStruct((r, d), F32) for r in group_rows],
        compiler_params=_cparams(("arbitrary",)),
        name="mlp_final",
    )(h2d, g.reshape(1, d), w1, w2, gf.reshape(1, d))


def _rotary_tables(s):
    half = ROT_DIM // 2
    inv = ROPE_THETA ** (-jnp.arange(half, dtype=F32) * 2.0 / ROT_DIM)
    ang = jnp.arange(s, dtype=F32)[:, None] * inv[None, :]
    cos, sin = jnp.cos(ang), jnp.sin(ang)
    pad = jnp.zeros((s, HEAD - ROT_DIM), F32)
    zero = jnp.zeros((s, half), F32)
    cos_t = jnp.concatenate([cos, cos, pad + 1.0], axis=1)
    sin_lo = jnp.concatenate([-sin, zero, pad], axis=1)
    sin_hi = jnp.concatenate([zero, sin, pad], axis=1)
    return cos_t, sin_lo, sin_hi


def _encode(xs, mems, mix_norm_g, w_in, hgrn_lb_logits, hgrn_gnorm_g, w_hgrn_o, w_attn_o, w_out,
            xa_norm_g, mem_norm_g, w_xq, w_xkv, w_xo, ffn_norm_g, w_ffn1, w_ffn2, final_norm_g,
            hgrn_rows=2048):
    s, d = xs[0].shape[1:]
    bsz = sum(x.shape[0] for x in xs)
    t = bsz * s
    depth = w_in.shape[0]
    assert depth == 1, "the final norm is fused into the (single) layer's MLP call"
    l = 0
    fd = HG_HEADS * HEAD
    lb_all = jnp.cumsum(jax.nn.softmax(hgrn_lb_logits.astype(F32), axis=1), axis=1)
    sizes = (fd,) * 5 + (ATTN_WIDTH,) * 9 + (d, d)
    offs = np.concatenate([[0], np.cumsum(sizes)])
    cos_t, sin_lo, sin_hi = _rotary_tables(s)
    tm = min(TOKEN_TILE, s)
    tmp = min(PROJ_TOKEN_TILE, s)
    tabspec = pl.BlockSpec((tmp, HEAD), lambda i: (i % (s // tmp), 0))
    srcs = [x.reshape(-1, d) for x in xs]
    mem = jnp.concatenate(mems, axis=0) if len(mems) > 1 else mems[0]

    wl = w_in[l]
    seg = lambda p: wl[:, offs[p]:offs[p + 1]]
    bf = lambda a: a.astype(BF16)
    w_silu = bf(jnp.concatenate([seg(0), seg(4)], axis=1))
    w_lf = bf(jnp.concatenate([seg(1), seg(2)], axis=1))
    w_plain = bf(jnp.concatenate([seg(3), seg(7), seg(10), seg(13)], axis=1))
    w_rot = bf(jnp.concatenate([seg(5), seg(6), seg(8), seg(9), seg(11), seg(12)], axis=1))
    w_gate = bf(jnp.concatenate([seg(14), seg(15)], axis=1))
    g_mix = mix_norm_g[l]

    a_silu, u = _first_proj(srcs, g_mix, w_silu, _ep_silu, BF16, tmp, "proj_silu")
    proj = functools.partial(_norm_proj, u, g_mix, tm=tmp, norm=False)
    lb_row = lb_all[:, l].reshape(1, 2 * fd)
    a_lf = proj(w_lf, _ep_logf, F32, extras=(lb_row,),
                extra_specs=(pl.BlockSpec((1, 2 * fd), lambda i: (0, 0)),), name="proj_logf")
    a_plain = proj(w_plain, _ep_plain, BF16, name="proj_plain")
    qscale = jnp.tile(jnp.concatenate([jnp.full((ATTN_WIDTH,), HEAD ** -0.5, F32),
                                       jnp.ones((ATTN_WIDTH,), F32)]), 3).reshape(1, 6 * ATTN_WIDTH)
    a_rot = proj(w_rot, _ep_rotary, BF16, extras=(cos_t, sin_lo, sin_hi, qscale),
                 extra_specs=(tabspec, tabspec, tabspec, pl.BlockSpec((1, 6 * ATTN_WIDTH), lambda i: (0, 0))),
                 name="proj_rotary")
    a_gate = proj(w_gate, _ep_sigmoid, BF16, name="proj_gate")

    a_silu3 = a_silu.reshape(bsz, s, 2 * fd)
    a_lf3 = a_lf.reshape(bsz, s, 2 * fd)
    a_plain3 = a_plain.reshape(bsz, s, -1)
    o_fwd = _hgrn_pass(a_silu3, a_lf3, a_plain3, hgrn_gnorm_g[l], None, False, hgrn_rows)
    hg = _hgrn_pass(a_silu3, a_lf3, a_plain3, hgrn_gnorm_g[l], o_fwd, True, hgrn_rows)

    attn = _dilated_attention(a_rot.reshape(bsz, s, -1), a_plain3)

    nm = mem.shape[1]
    kv = _norm_proj(mem.reshape(bsz * nm, d), mem_norm_g[l], w_xkv[l].astype(BF16), _ep_plain, BF16,
                    tm=nm, name="proj_memkv")
    h2d = _merge_xattn(srcs, hg.reshape(t, fd), attn.reshape(t, ATTN_WIDTH), a_gate,
                       w_hgrn_o[l].astype(BF16), w_attn_o[l].astype(BF16), w_out[l].astype(BF16),
                       xa_norm_g[l], w_xq[l].astype(BF16), kv.reshape(bsz, nm, 2 * d), w_xo[l].astype(BF16),
                       s, tm=tm)
    outs = _mlp(h2d, ffn_norm_g[l], w_ffn1[l].astype(BF16), w_ffn2[l].astype(BF16), final_norm_g,
                [x.shape[0] * s for x in xs], tm=tm)
    return [o.reshape(x.shape) for o, x in zip(outs, xs)]


def kernel(x_prompt, x_sample, mem_prompt, mem_sample, mix_norm_g, w_in, hgrn_lb_logits, hgrn_gnorm_g,
           w_hgrn_o, w_attn_o, w_out, xa_norm_g, mem_norm_g, w_xq, w_xkv, w_xo, ffn_norm_g, w_ffn1,
           w_ffn2, final_norm_g):
    assert x_prompt.shape[1:] == x_sample.shape[1:]
    y_prompt, y_sample = _encode(
        [x_prompt, x_sample], [mem_prompt, mem_sample], mix_norm_g, w_in, hgrn_lb_logits, hgrn_gnorm_g,
        w_hgrn_o, w_attn_o, w_out, xa_norm_g, mem_norm_g, w_xq, w_xkv, w_xo, ffn_norm_g, w_ffn1, w_ffn2,
        final_norm_g)
    return y_prompt, y_sample
```
